```python
import math
import jax
import jax.numpy as jnp
from jax import lax
import numpy as np

D_MODEL = 1024
BATCH = 8
SEQ = 4096
DEPTH = 4

GRID_W = 64
CTX_LEN = 256
EPS = 1e-6
N_MOD = 9
D_FF = 2816

SSD_EXPAND = 2
D_INNER = SSD_EXPAND * D_MODEL
SSD_HEAD_DIM = 64
SSD_HEADS = D_INNER // SSD_HEAD_DIM
SSD_GROUPS = 4
SSD_HEADS_PER_GROUP = SSD_HEADS // SSD_GROUPS
D_STATE = 128
D_CONV = 3
SSD_CHUNK = 128
BC_DIM = SSD_GROUPS * D_STATE
CONV_DIM = D_INNER + 2 * BC_DIM

ATT_HEAD_DIM = 128
N_Q_HEADS = D_MODEL // ATT_HEAD_DIM
N_KV_HEADS = 2
Q_PER_KV = N_Q_HEADS // N_KV_HEADS
ATT_DIM = N_Q_HEADS * ATT_HEAD_DIM
KV_DIM = N_KV_HEADS * ATT_HEAD_DIM
ATT_SCALE = ATT_HEAD_DIM ** -0.5
Q_BLOCK = 128
ROPE_THETA = 10000.0
ROPE_AXIS_DIM = ATT_HEAD_DIM // 2

IN_SPLITS = (D_INNER, CONV_DIM, SSD_HEADS, SSD_HEADS, ATT_DIM, KV_DIM, KV_DIM, D_MODEL, D_MODEL)
IN_DIM = sum(IN_SPLITS)

kernel_name = "hybrid_ssd_gqa_prefix_dit_block"


def split_cols(u, sizes):
    out, start = [], 0
    for s in sizes:
        out.append(u[..., start:start + s])
        start += s
    return out


def rms_norm(x, g):
    xf = x.astype(jnp.float32)
    xf = xf * lax.rsqrt(jnp.mean(xf * xf, axis=-1, keepdims=True) + EPS)
    return (xf * g.astype(jnp.float32)).astype(x.dtype)


def group_rms_norm(y, g, n_groups):
    yf = y.astype(jnp.float32).reshape(*y.shape[:-1], n_groups, y.shape[-1] // n_groups)
    yf = yf * lax.rsqrt(jnp.mean(yf * yf, axis=-1, keepdims=True) + EPS)
    return (yf.reshape(y.shape) * g.astype(jnp.float32)).astype(y.dtype)


def ada_params(cond, w, b):
    m = (jax.nn.silu(cond) @ w + b)[..., None, :]
    return jnp.split(m, N_MOD, axis=-1)


def modulate(h, shift, scale):
    return h * (1.0 + scale) + shift


def swiglu(h, w13, w2):
    a, g = jnp.split(h @ w13, 2, axis=-1)
    return (jax.nn.silu(a) * g) @ w2


def centred_dwconv(u, w, bias):
    out = lax.conv_general_dilated(
        u, w[:, None, :].astype(u.dtype), window_strides=(1,),
        padding=[(D_CONV // 2, D_CONV // 2)], dimension_numbers=("NWC", "WIO", "NWC"),
        feature_group_count=u.shape[-1])
    return out + bias


def segsum(a):
    cs = jnp.cumsum(a, axis=-1)
    seg = cs[..., :, None] - cs[..., None, :]
    n = a.shape[-1]
    mask = jnp.tril(jnp.ones((n, n), dtype=bool))
    return jnp.where(mask, seg, -jnp.inf)


def ssd_chunked(X, dA, Bm, Cm, init_state, with_output):
    b, l, h, p = X.shape
    g, n = Bm.shape[-2:]
    r = h // g
    nc = l // SSD_CHUNK
    Xc = X.reshape(b, nc, SSD_CHUNK, g, r, p)
    Bc = Bm.reshape(b, nc, SSD_CHUNK, g, n)
    Cc = Cm.reshape(b, nc, SSD_CHUNK, g, n)
    A = dA.reshape(b, nc, SSD_CHUNK, g, r).transpose(0, 1, 3, 4, 2)
    A_cs = jnp.cumsum(A, axis=-1)
    decay_to_end = jnp.exp(A_cs[..., -1:] - A_cs).transpose(0, 1, 4, 2, 3)
    chunk_states = jnp.einsum("bclgn,bclgrp->cbgrpn", Bc, Xc * decay_to_end[..., None])
    chunk_decay = jnp.exp(A_cs[..., -1]).transpose(1, 0, 2, 3)

    def carry_step(state, inp):
        s_chunk, d_chunk = inp
        return state * d_chunk[..., None, None] + s_chunk, state

    final, entering = lax.scan(carry_step, init_state, (chunk_states, chunk_decay))
    if not with_output:
        return None, final
    CB = jnp.einsum("bclgn,bcsgn->bcgls", Cc, Bc)
    M = CB[:, :, :, None] * jnp.exp(segsum(A))
    y_diag = jnp.einsum("bcgrls,bcsgrp->bclgrp", M, Xc)
    y_off = jnp.einsum("bclgn,cbgrpn->bclgrp", Cc, entering) * jnp.exp(A_cs).transpose(0, 1, 4, 2, 3)[..., None]
    return (y_diag + y_off).reshape(b, l, h, p), final


def ssd_branch(z, xbc, dt_raw_f, dt_raw_b, conv_w, conv_b, a_log, dt_bias, d_skip, norm_g, w_proj,
               init_f, init_b, with_output):
    b, l = z.shape[:2]
    xbc = jax.nn.silu(centred_dwconv(xbc, conv_w, conv_b))
    xs, Bm, Cm = split_cols(xbc, (D_INNER, BC_DIM, BC_DIM))
    xs = xs.reshape(b, l, SSD_HEADS, SSD_HEAD_DIM)
    Bm = Bm.reshape(b, l, SSD_GROUPS, D_STATE)
    Cm = Cm.reshape(b, l, SSD_GROUPS, D_STATE)
    ys, finals = [], []
    for d, (dt_raw, init, rev) in enumerate(((dt_raw_f, init_f, False), (dt_raw_b, init_b, True))):
        dt = jax.nn.softplus(dt_raw.astype(jnp.float32) + dt_bias[d].astype(jnp.float32))
        dA = dt * (-jnp.exp(a_log[d].astype(jnp.float32)))
        X = xs * dt[..., None]
        Bd, Cd = Bm, Cm
        if rev:
            X, dA, Bd, Cd = X[:, ::-1], dA[:, ::-1], Bd[:, ::-1], Cd[:, ::-1]
        y, fin = ssd_chunked(X, dA, Bd, Cd, init, with_output)
        finals.append(fin)
        if with_output:
            ys.append(y[:, ::-1] if rev else y)
    if not with_output:
        return None, finals[0], finals[1]
    y = (ys[0] + ys[1]).astype(z.dtype) + xs * d_skip[:, None]
    y = y.reshape(b, l, D_INNER) * jax.nn.silu(z)
    return group_rms_norm(y, norm_g, SSD_GROUPS) @ w_proj, finals[0], finals[1]


def axial_rope_tables(n_tok):
    rows = n_tok // GRID_W
    row = jnp.broadcast_to(jnp.arange(rows)[:, None], (rows, GRID_W)).reshape(-1).astype(jnp.float32)
    col = jnp.broadcast_to(jnp.arange(GRID_W)[None, :], (rows, GRID_W)).reshape(-1).astype(jnp.float32)
    inv_freq = ROPE_THETA ** (-(jnp.arange(0, ROPE_AXIS_DIM, 2, dtype=jnp.float32) / ROPE_AXIS_DIM))
    ang = jnp.stack([row[:, None] * inv_freq, col[:, None] * inv_freq], axis=1)
    return jnp.cos(ang), jnp.sin(ang)


def apply_axial_rope(x, cos, sin):
    xr = x.astype(jnp.float32).reshape(*x.shape[:-1], 2, 2, ATT_HEAD_DIM // 4)
    x1, x2 = xr[..., 0, :], xr[..., 1, :]
    c, s = cos[None, :, None], sin[None, :, None]
    out = jnp.stack([x1 * c - x2 * s, x2 * c + x1 * s], axis=-2)
    return out.reshape(x.shape).astype(x.dtype)


def attn_heads(q, k, v, qk_norm_g):
    b, l = q.shape[:2]
    q = rms_norm(q.reshape(b, l, N_Q_HEADS, ATT_HEAD_DIM), qk_norm_g[0])
    k = rms_norm(k.reshape(b, l, N_KV_HEADS, ATT_HEAD_DIM), qk_norm_g[1])
    return q, k, v.reshape(b, l, N_KV_HEADS, ATT_HEAD_DIM)


def gqa_core(qg, k, v):
    s = jnp.einsum("bqkgd,bskd->bkgqs", qg, k).astype(jnp.float32) * ATT_SCALE
    p = jax.nn.softmax(s, axis=-1).astype(v.dtype)
    return jnp.einsum("bkgqs,bskd->bqkgd", p, v)


def blocked_latent_attention(q, k_all, v_all):
    b, l = q.shape[:2]
    nb = l // Q_BLOCK
    qb = q.reshape(b, nb, Q_BLOCK, N_KV_HEADS, Q_PER_KV, ATT_HEAD_DIM).transpose(1, 0, 2, 3, 4, 5)
    ob = lax.map(lambda blk: gqa_core(blk, k_all, v_all), qb)
    return ob.transpose(1, 0, 2, 3, 4, 5).reshape(b, l, ATT_DIM)


def token_mixer(h, hc, w_in, conv_w, conv_b, a_log, dt_bias, d_skip, ssd_norm_g, w_ssd_out,
                qk_norm_g, w_attn_out, w_out, cos, sin, with_ctx_out):
    b = h.shape[0]
    z, xbc, dtf, dtb, q, k, v, g_ssd, g_att = split_cols(h @ w_in, IN_SPLITS)
    zc, xbcc, dtfc, dtbc, qc, kc, vc, g_ssd_c, g_att_c = split_cols(hc @ w_in, IN_SPLITS)
    ssd_w = (conv_w, conv_b, a_log, dt_bias, d_skip, ssd_norm_g, w_ssd_out)
    zero_state = jnp.zeros((b, SSD_GROUPS, SSD_HEADS_PER_GROUP, SSD_HEAD_DIM, D_STATE), jnp.float32)
    yc_ssd, state_f, state_b = ssd_branch(zc, xbcc, dtfc, dtbc, *ssd_w, zero_state, zero_state, with_ctx_out)
    y_ssd, _, _ = ssd_branch(z, xbc, dtf, dtb, *ssd_w, state_f, state_b, True)

    qc, kc, vc = attn_heads(qc, kc, vc, qk_norm_g)
    q, k, v = attn_heads(q, k, v, qk_norm_g)
    q = apply_axial_rope(q, cos, sin)
    k = apply_axial_rope(k, cos, sin)
    k_all = jnp.concatenate([kc, k], axis=1)
    v_all = jnp.concatenate([vc, v], axis=1)
    y_att = blocked_latent_attention(q, k_all, v_all) @ w_attn_out
    u = (jax.nn.sigmoid(g_ssd) * y_ssd + jax.nn.sigmoid(g_att) * y_att) @ w_out
    if not with_ctx_out:
        return u, None
    bc, lc = qc.shape[:2]
    yc_att = gqa_core(qc.reshape(bc, lc, N_KV_HEADS, Q_PER_KV, ATT_HEAD_DIM), kc, vc).reshape(bc, lc, ATT_DIM) @ w_attn_out
    uc = (jax.nn.sigmoid(g_ssd_c) * yc_ssd + jax.nn.sigmoid(g_att_c) * yc_att) @ w_out
    return u, uc


def setup_inputs(seed: int = 0) -> dict:
    key = jax.random.key(seed)
    ks = jax.random.split(key, 24)
    L, D = DEPTH, D_MODEL

    def nrm(k, shape, scale):
        return jax.random.normal(k, shape, jnp.float32) * scale

    dt0 = jnp.exp(jax.random.uniform(ks[13], (L, 2, SSD_HEADS), jnp.float32, math.log(1e-3), math.log(1e-1)))
    return {
        "x": nrm(ks[0], (BATCH, SEQ, D), 1.0),
        "c": nrm(ks[1], (BATCH, D), 1.0),
        "ctx": nrm(ks[2], (BATCH, CTX_LEN, D), 1.0),
        "c_ctx": nrm(ks[3], (D,), 1.0),
        "w_ada": nrm(ks[4], (L, D, N_MOD * D), 0.2 * D ** -0.5),
        "b_ada": nrm(ks[5], (L, N_MOD * D), 0.02),
        "norm_g": 1.0 + nrm(ks[6], (L, 3, D), 0.05),
        "ffn1_w13": nrm(ks[7], (L, D, 2 * D_FF), D ** -0.5),
        "ffn1_w2": nrm(ks[8], (L, D_FF, D), D_FF ** -0.5),
        "w_in": nrm(ks[9], (L, D, IN_DIM), D ** -0.5),
        "conv_w": nrm(ks[10], (L, D_CONV, CONV_DIM), D_CONV ** -0.5),
        "conv_b": nrm(ks[11], (L, CONV_DIM), 0.02),
        "a_log": jnp.log(jax.random.uniform(ks[12], (L, 2, SSD_HEADS), jnp.float32, 1.0, 16.0)),
        "dt_bias": dt0 + jnp.log(-jnp.expm1(-dt0)),
        "d_skip": 1.0 + nrm(ks[14], (L, SSD_HEADS), 0.1),
        "ssd_norm_g": 1.0 + nrm(ks[15], (L, D_INNER), 0.05),
        "w_ssd_out": nrm(ks[16], (L, D_INNER, D), D_INNER ** -0.5),
        "qk_norm_g": 1.0 + nrm(ks[17], (L, 2, ATT_HEAD_DIM), 0.05),
        "w_attn_out": nrm(ks[18], (L, ATT_DIM, D), ATT_DIM ** -0.5),
        "w_out": nrm(ks[19], (L, D, D), D ** -0.5),
        "ffn2_w13": nrm(ks[20], (L, D, 2 * D_FF), D ** -0.5),
        "ffn2_w2": nrm(ks[21], (L, D_FF, D), D_FF ** -0.5),
    }


def reference(x, c, ctx, c_ctx, w_ada, b_ada, norm_g, ffn1_w13, ffn1_w2, w_in, conv_w, conv_b,
              a_log, dt_bias, d_skip, ssd_norm_g, w_ssd_out, qk_norm_g, w_attn_out, w_out,
              ffn2_w13, ffn2_w2):
    cos, sin = axial_rope_tables(x.shape[1])
    xc = ctx
    for i in range(DEPTH):
        last = i == DEPTH - 1
        sh1, sc1, g1, sh2, sc2, g2, sh3, sc3, g3 = ada_params(c, w_ada[i], b_ada[i])
        csh1, csc1, cg1, csh2, csc2, cg2, csh3, csc3, cg3 = ada_params(c_ctx, w_ada[i], b_ada[i])
        x = x + 0.5 * g1 * swiglu(modulate(rms_norm(x, norm_g[i, 0]), sh1, sc1), ffn1_w13[i], ffn1_w2[i])
        xc = xc + 0.5 * cg1 * swiglu(modulate(rms_norm(xc, norm_g[i, 0]), csh1, csc1), ffn1_w13[i], ffn1_w2[i])
        h = modulate(rms_norm(x, norm_g[i, 1]), sh2, sc2)
        hc = modulate(rms_norm(xc, norm_g[i, 1]), csh2, csc2)
        u, uc = token_mixer(h, hc, w_in[i], conv_w[i], conv_b[i], a_log[i], dt_bias[i], d_skip[i],
                            ssd_norm_g[i], w_ssd_out[i], qk_norm_g[i], w_attn_out[i], w_out[i],
                            cos, sin, not last)
        x = x + g2 * u
        x = x + 0.5 * g3 * swiglu(modulate(rms_norm(x, norm_g[i, 2]), sh3, sc3), ffn2_w13[i], ffn2_w2[i])
        if not last:
            xc = xc + cg2 * uc
            xc = xc + 0.5 * cg3 * swiglu(modulate(rms_norm(xc, norm_g[i, 2]), csh3, csc3), ffn2_w13[i], ffn2_w2[i])
    return x
```

```python
import functools
import math

import jax
import jax.numpy as jnp
from jax import lax
from jax.experimental import pallas as pl
from jax.experimental.pallas import tpu as pltpu

F32 = jnp.float32
BF16 = jnp.bfloat16

D_MODEL = 1024
DEPTH = 4
GRID_W = 64
EPS = 1e-6
N_MOD = 9
D_FF = 2816

D_INNER = 2048
SSD_HEAD_DIM = 64
SSD_HEADS = 32
SSD_GROUPS = 4
HEADS_PER_GROUP = SSD_HEADS // SSD_GROUPS
GROUP_DIM = HEADS_PER_GROUP * SSD_HEAD_DIM
D_STATE = 128
BC_DIM = SSD_GROUPS * D_STATE
CONV_DIM = D_INNER + 2 * BC_DIM
CHUNK = 128

ATT_HEAD_DIM = 128
N_Q_HEADS = 8
N_KV_HEADS = 2
Q_PER_KV = N_Q_HEADS // N_KV_HEADS
ATT_DIM = N_Q_HEADS * ATT_HEAD_DIM
KV_DIM = N_KV_HEADS * ATT_HEAD_DIM
ATT_SCALE = ATT_HEAD_DIM ** -0.5
ROPE_THETA = 10000.0
ROPE_AXIS_DIM = ATT_HEAD_DIM // 2

LANES = 128
DT_COLS = SSD_GROUPS * LANES
OFF_Z = 0
OFF_XBC = OFF_Z + D_INNER
OFF_Q = OFF_XBC + CONV_DIM
OFF_K = OFF_Q + ATT_DIM
OFF_V = OFF_K + KV_DIM
OFF_GS = OFF_V + KV_DIM
OFF_GA = OFF_GS + D_MODEL
OFF_DT = OFF_GA + D_MODEL
IN_COLS = OFF_DT + DT_COLS

MOD_ROWS = 16
VMEM_LIMIT = 56 * 1024 * 1024


def _cparams(sem):
    return pltpu.CompilerParams(dimension_semantics=sem, vmem_limit_bytes=VMEM_LIMIT)


def _silu(v):
    return v * jax.nn.sigmoid(v)


def _norm_mod(x, ng, sc, sh):
    r = lax.rsqrt(jnp.mean(x * x, axis=-1, keepdims=True) + EPS)
    return (x * r * ng) * (1.0 + sc) + sh


def _ada_kernel(cond_ref, w_ref, b_ref, o_ref):
    a = _silu(cond_ref[...])
    o_ref[...] = jnp.dot(a, w_ref[...], preferred_element_type=F32,
                         precision=lax.Precision.HIGHEST) + b_ref[...]


def ada_all(cond, w_ada, b_ada, tn=1024):
    depth, d, n = w_ada.shape
    return pl.pallas_call(
        _ada_kernel,
        grid=(depth, n // tn),
        in_specs=[pl.BlockSpec((MOD_ROWS, d), lambda l, j: (0, 0)),
                  pl.BlockSpec((None, d, tn), lambda l, j: (l, 0, j)),
                  pl.BlockSpec((None, 1, tn), lambda l, j: (l, 0, j))],
        out_specs=pl.BlockSpec((None, MOD_ROWS, tn), lambda l, j: (l, 0, j)),
        out_shape=jax.ShapeDtypeStruct((depth, MOD_ROWS, n), F32),
        compiler_params=_cparams(("arbitrary", "arbitrary")),
        name="ada_params",
    )(cond, w_ada, b_ada.reshape(depth, 1, n))


def _mod_spec(row_of_b, k):
    return pl.BlockSpec((None, 1, D_MODEL), lambda b, i, *_: (row_of_b(b), 0, k))


def _ffn_kernel(x_ref, sh_ref, sc_ref, gt_ref, ng_ref, w1_ref, w3_ref, w2_ref, o_ref, h_scr, acc_scr):
    j = pl.program_id(2)

    @pl.when(j == 0)
    def _():
        h_scr[...] = _norm_mod(x_ref[...], ng_ref[...], sc_ref[...], sh_ref[...]).astype(BF16)

    h = h_scr[...]
    a = jnp.dot(h, w1_ref[...], preferred_element_type=F32)
    g = jnp.dot(h, w3_ref[...], preferred_element_type=F32)
    act = (_silu(a) * g).astype(BF16)
    part = jnp.dot(act, w2_ref[...], preferred_element_type=F32)

    @pl.when(j == 0)
    def _():
        acc_scr[...] = part

    @pl.when(j > 0)
    def _():
        acc_scr[...] += part

    @pl.when(j == pl.num_programs(2) - 1)
    def _():
        o_ref[...] = x_ref[...] + 0.5 * gt_ref[...] * acc_scr[...]


def ffn(x, mods, row_of_b, k0, ng, w13, w2, tm, tf=D_FF // 2):
    b, l, d = x.shape
    nj = D_FF // tf
    return pl.pallas_call(
        _ffn_kernel,
        grid=(b, l // tm, nj),
        in_specs=[pl.BlockSpec((None, tm, d), lambda b, i, j: (b, i, 0)),
                  _mod_spec(row_of_b, k0), _mod_spec(row_of_b, k0 + 1), _mod_spec(row_of_b, k0 + 2),
                  pl.BlockSpec((1, d), lambda b, i, j: (0, 0)),
                  pl.BlockSpec((d, tf), lambda b, i, j: (0, j)),
                  pl.BlockSpec((d, tf), lambda b, i, j: (0, nj + j)),
                  pl.BlockSpec((tf, d), lambda b, i, j: (j, 0))],
        out_specs=pl.BlockSpec((None, tm, d), lambda b, i, j: (b, i, 0)),
        out_shape=jax.ShapeDtypeStruct(x.shape, F32),
        scratch_shapes=[pltpu.VMEM((tm, d), BF16), pltpu.VMEM((tm, d), F32)],
        compiler_params=_cparams(("arbitrary", "arbitrary", "arbitrary")),
        name="ffn",
    )(x, mods, mods, mods, ng, w13, w13, w2)


def _swap32(v):
    lane = lax.broadcasted_iota(jnp.int32, v.shape, 1)
    return jnp.where((lane & 32) == 0, pltpu.roll(v, LANES - 32, axis=1), pltpu.roll(v, 32, axis=1))


def _inproj_kernel(x_ref, sh_ref, sc_ref, ng_ref, w_ref, qkg_ref, cos_ref, sin_ref,
                   z_ref, xbc_ref, q_ref, k_ref, v_ref, gs_ref, ga_ref, dt_ref):
    h = _norm_mod(x_ref[...], ng_ref[...], sc_ref[...], sh_ref[...]).astype(BF16)

    def proj(lo, hi):
        return jnp.dot(h, w_ref[:, lo:hi], preferred_element_type=F32)

    z_ref[...] = proj(OFF_Z, OFF_XBC).astype(BF16)
    xbc_ref[...] = proj(OFF_XBC, OFF_Q).astype(BF16)
    v_ref[...] = proj(OFF_V, OFF_GS).astype(BF16)
    gs_ref[...] = proj(OFF_GS, OFF_GA).astype(BF16)
    ga_ref[...] = proj(OFF_GA, OFF_DT).astype(BF16)
    dt_ref[...] = proj(OFF_DT, IN_COLS)

    cos = cos_ref[...]
    sin = sin_ref[...]

    def head(col, gain, scale):
        t = proj(col, col + ATT_HEAD_DIM)
        t = t * lax.rsqrt(jnp.mean(t * t, axis=-1, keepdims=True) + EPS) * gain
        t = t * cos + _swap32(t) * sin
        return (t * scale).astype(BF16) if scale != 1.0 else t.astype(BF16)

    for hh in range(N_Q_HEADS):
        q_ref[:, hh * ATT_HEAD_DIM:(hh + 1) * ATT_HEAD_DIM] = head(OFF_Q + hh * ATT_HEAD_DIM, qkg_ref[0:1, :], ATT_SCALE)
    for hh in range(N_KV_HEADS):
        k_ref[:, hh * ATT_HEAD_DIM:(hh + 1) * ATT_HEAD_DIM] = head(OFF_K + hh * ATT_HEAD_DIM, qkg_ref[1:2, :], 1.0)


def inproj(x, mods, row_of_b, ng, w_in, qk_g, cos_t, sin_t, tm):
    b, l, d = x.shape
    tok = lambda b, i: (b, i, 0)
    const = lambda b, i: (0, 0)
    widths = (D_INNER, CONV_DIM, ATT_DIM, KV_DIM, KV_DIM, D_MODEL, D_MODEL, DT_COLS)
    dtypes = (BF16, BF16, BF16, BF16, BF16, BF16, BF16, F32)
    return pl.pallas_call(
        _inproj_kernel,
        grid=(b, l // tm),
        in_specs=[pl.BlockSpec((None, tm, d), tok),
                  _mod_spec(row_of_b, 3), _mod_spec(row_of_b, 4),
                  pl.BlockSpec((1, d), const),
                  pl.BlockSpec((d, IN_COLS), const, pipeline_mode=pl.Buffered(1)),
                  pl.BlockSpec((2, ATT_HEAD_DIM), const),
                  pl.BlockSpec((tm, ATT_HEAD_DIM), lambda b, i: (i, 0)),
                  pl.BlockSpec((tm, ATT_HEAD_DIM), lambda b, i: (i, 0))],
        out_specs=[pl.BlockSpec((None, tm, w), tok) for w in widths],
        out_shape=[jax.ShapeDtypeStruct((b, l, w), dt) for w, dt in zip(widths, dtypes)],
        compiler_params=_cparams(("arbitrary", "arbitrary")),
        name="inproj",
    )(x, mods, mods, ng, w_in, qk_g, cos_t, sin_t)


def _scan_rows(v, suffix):
    n = v.shape[0]
    row = lax.broadcasted_iota(jnp.int32, v.shape, 0)
    k = 1
    while k < n:
        if suffix:
            v = v + jnp.where(row < n - k, pltpu.roll(v, n - k, axis=0), 0.0)
        else:
            v = v + jnp.where(row >= k, pltpu.roll(v, k, axis=0), 0.0)
        k *= 2
    return v


def _expand_heads(a, off):
    lane = lax.broadcasted_iota(jnp.int32, (a.shape[0], LANES), 1)
    blocks = []
    for j in range(HEADS_PER_GROUP // 2):
        lo = a[:, off + 2 * j:off + 2 * j + 1]
        hi = a[:, off + 2 * j + 1:off + 2 * j + 2]
        blocks.append(jnp.where(lane < SSD_HEAD_DIM, lo, hi))
    return jnp.concatenate(blocks, axis=1)


def _dt_prep(raw, bias_row, nega_row):
    v = raw + bias_row
    dt = jnp.maximum(v, 0.0) + jnp.log1p(jnp.exp(-jnp.abs(v)))
    da = dt * nega_row
    lane = lax.broadcasted_iota(jnp.int32, raw.shape, 1)
    pre = _scan_rows(da, suffix=False)
    cs = jnp.where(lane < HEADS_PER_GROUP, pre, _scan_rows(da, suffix=True))
    tot = pre[CHUNK - 1:CHUNK, :]
    return dt, cs, tot


def _ssd_kernel(with_ctx_out,
                xs_l, bm_l, cm_l, z_l, dt_l, xs_c, bm_c, cm_c, z_c, dt_c,
                cwx, cwb, cwc, cbx, cbb, cbc, dtb, nega, dskip, ng,
                y_l, y_c,
                xc_l, bt_l, cc_l, sb_l, xc_c, bt_c, cc_c, sb_c, st_f, st_b):
    lat = dict(xs=xs_l, bm=bm_l, cm=cm_l, z=z_l, dt=dt_l, y=y_l, xc=xc_l, bt=bt_l, cc=cc_l, sb=sb_l, out=True)
    ctx = dict(xs=xs_c, bm=bm_c, cm=cm_c, z=z_c, dt=dt_c, y=y_c, xc=xc_c, bt=bt_c, cc=cc_c, sb=sb_c,
               out=with_ctx_out)
    bias_row = dtb[...]
    nega_row = nega[...]

    def rows(c):
        return pl.ds(pl.multiple_of(c * CHUNK, CHUNK), CHUNK)

    def conv_pass(s):
        n_chunks = s["xs"].shape[0] // CHUNK
        row = lax.broadcasted_iota(jnp.int32, (CHUNK, 1), 0)

        def conv(src, w_ref, b_ref, c):
            cur = src[rows(c), :].astype(F32)
            p_at = pl.multiple_of(jnp.maximum(c * CHUNK - 16, 0), 16)
            n_at = pl.multiple_of(jnp.minimum((c + 1) * CHUNK, (n_chunks - 1) * CHUNK + CHUNK - 16), 16)
            prev = src[pl.ds(p_at, 16), :].astype(F32)[15:16, :] * (c > 0).astype(F32)
            nxt = src[pl.ds(n_at, 16), :].astype(F32)[0:1, :] * (c < n_chunks - 1).astype(F32)
            up = jnp.where(row == 0, prev, pltpu.roll(cur, 1, axis=0))
            dn = jnp.where(row == CHUNK - 1, nxt, pltpu.roll(cur, CHUNK - 1, axis=0))
            return _silu(w_ref[0:1, :] * up + w_ref[1:2, :] * cur + w_ref[2:3, :] * dn + b_ref[...])

        def body(c, carry):
            s["xc"][rows(c), :] = conv(s["xs"], cwx, cbx, c).astype(BF16)
            s["cc"][rows(c), :] = conv(s["cm"], cwc, cbc, c).astype(BF16)
            s["bt"][c] = conv(s["bm"], cwb, cbb, c).T.astype(BF16)
            return carry

        lax.fori_loop(0, n_chunks, body, 0)

    def bwd_pass(s):
        n_chunks = s["xs"].shape[0] // CHUNK

        def body(i, carry):
            c = n_chunks - 1 - i
            s["sb"][c] = st_b[...].astype(BF16)
            dt, cs, tot = _dt_prep(s["dt"][rows(c), :], bias_row, nega_row)
            w = dt * jnp.exp(tot - cs)
            xw = (s["xc"][rows(c), :].astype(F32) * _expand_heads(w, HEADS_PER_GROUP)).astype(BF16)
            st_b[...] = st_b[...] * _expand_heads(jnp.exp(tot), HEADS_PER_GROUP) + jnp.dot(
                s["bt"][c], xw, preferred_element_type=F32)
            return carry

        lax.fori_loop(0, n_chunks, body, 0)

    def fwd_pass(s):
        n_chunks = s["xs"].shape[0] // CHUNK
        ri = lax.broadcasted_iota(jnp.int32, (CHUNK, CHUNK), 0)
        ci = lax.broadcasted_iota(jnp.int32, (CHUNK, CHUNK), 1)
        lower = ri >= ci
        upper = ri <= ci
        lane = lax.broadcasted_iota(jnp.int32, (CHUNK, LANES), 1)
        first = lane < SSD_HEAD_DIM

        def body(c, carry):
            dt, cs, tot = _dt_prep(s["dt"][rows(c), :], bias_row, nega_row)
            xc = s["xc"][rows(c), :]
            bt = s["bt"][c]
            if s["out"]:
                cc = s["cc"][rows(c), :]
                e = jnp.exp(cs)
                pk_t = jnp.where(lane < 2 * HEADS_PER_GROUP, cs, pltpu.roll(dt, 2 * HEADS_PER_GROUP, axis=1)).T
                cb = jnp.dot(cc, bt, preferred_element_type=F32)
                y_f = jnp.dot(cc, st_f[...].astype(BF16), preferred_element_type=F32)
                y_b = jnp.dot(cc, s["sb"][c], preferred_element_type=F32)
                blocks = []
                for j in range(HEADS_PER_GROUP // 2):
                    ms = []
                    for hh in (2 * j, 2 * j + 1):
                        hb = HEADS_PER_GROUP + hh
                        d_f = jnp.exp(jnp.where(lower, cs[:, hh:hh + 1] - pk_t[hh:hh + 1, :], -jnp.inf))
                        d_b = jnp.exp(jnp.where(upper, cs[:, hb:hb + 1] - pk_t[hb:hb + 1, :], -jnp.inf))
                        m = cb * (d_f * pk_t[16 + hh:17 + hh, :] + d_b * pk_t[16 + hb:17 + hb, :])
                        ms.append(m.astype(BF16))
                    xblk = xc[:, j * LANES:(j + 1) * LANES]
                    zero = jnp.zeros_like(xblk)
                    x_bd = jnp.concatenate([jnp.where(first, xblk, zero), jnp.where(first, zero, xblk)], axis=0)
                    yd = jnp.dot(jnp.concatenate(ms, axis=1), x_bd, preferred_element_type=F32)
                    e_f = jnp.where(first, e[:, 2 * j:2 * j + 1], e[:, 2 * j + 1:2 * j + 2])
                    e_b = jnp.where(first, e[:, 8 + 2 * j:9 + 2 * j], e[:, 9 + 2 * j:10 + 2 * j])
                    sl = slice(j * LANES, (j + 1) * LANES)
                    blocks.append(yd + y_f[:, sl] * e_f + y_b[:, sl] * e_b + xblk.astype(F32) * dskip[:, sl])
                y = jnp.concatenate(blocks, axis=1) * _silu(s["z"][rows(c), :].astype(F32))
                y = y * lax.rsqrt(jnp.mean(y * y, axis=-1, keepdims=True) + EPS) * ng[...]
                s["y"][rows(c), :] = y.astype(BF16)
            w = dt * jnp.exp(tot - cs)
            xw = (xc.astype(F32) * _expand_heads(w, 0)).astype(BF16)
            st_f[...] = st_f[...] * _expand_heads(jnp.exp(tot), 0) + jnp.dot(bt, xw, preferred_element_type=F32)
            return carry

        lax.fori_loop(0, n_chunks, body, 0)

    conv_pass(ctx)
    conv_pass(lat)
    st_b[...] = jnp.zeros_like(st_b)
    bwd_pass(ctx)
    bwd_pass(lat)
    st_f[...] = jnp.zeros_like(st_f)
    fwd_pass(ctx)
    fwd_pass(lat)
    if not with_ctx_out:
        y_c[...] = jnp.zeros_like(y_c)


def ssd(xbc_l, z_l, dt_l, xbc_c, z_c, dt_c, conv_w, conv_b, dtb_rows, nega_rows, dskip_row, ng_row, with_ctx_out):
    b, l, _ = xbc_l.shape
    lc = xbc_c.shape[1]
    n_xg = D_INNER // GROUP_DIM
    n_bg = D_INNER // D_STATE

    def seq_specs(n):
        return [pl.BlockSpec((None, n, GROUP_DIM), lambda b, g: (b, 0, g)),
                pl.BlockSpec((None, n, D_STATE), lambda b, g: (b, 0, n_bg + g)),
                pl.BlockSpec((None, n, D_STATE), lambda b, g: (b, 0, n_bg + SSD_GROUPS + g)),
                pl.BlockSpec((None, n, GROUP_DIM), lambda b, g: (b, 0, g)),
                pl.BlockSpec((None, n, LANES), lambda b, g: (b, 0, g))]

    def seq_scratch(n):
        return [pltpu.VMEM((n, GROUP_DIM), BF16), pltpu.VMEM((n // CHUNK, D_STATE, CHUNK), BF16),
                pltpu.VMEM((n, D_STATE), BF16), pltpu.VMEM((n // CHUNK, D_STATE, GROUP_DIM), BF16)]

    par_specs = [pl.BlockSpec((3, GROUP_DIM), lambda b, g: (0, g)),
                 pl.BlockSpec((3, D_STATE), lambda b, g: (0, n_bg + g)),
                 pl.BlockSpec((3, D_STATE), lambda b, g: (0, n_bg + SSD_GROUPS + g)),
                 pl.BlockSpec((1, GROUP_DIM), lambda b, g: (0, g)),
                 pl.BlockSpec((1, D_STATE), lambda b, g: (0, n_bg + g)),
                 pl.BlockSpec((1, D_STATE), lambda b, g: (0, n_bg + SSD_GROUPS + g)),
                 pl.BlockSpec((None, 1, LANES), lambda b, g: (g, 0, 0)),
                 pl.BlockSpec((None, 1, LANES), lambda b, g: (g, 0, 0)),
                 pl.BlockSpec((1, GROUP_DIM), lambda b, g: (0, g)),
                 pl.BlockSpec((1, GROUP_DIM), lambda b, g: (0, g))]
    del n_xg
    return pl.pallas_call(
        functools.partial(_ssd_kernel, with_ctx_out),
        grid=(b, SSD_GROUPS),
        in_specs=seq_specs(l) + seq_specs(lc) + par_specs,
        out_specs=[pl.BlockSpec((None, l, GROUP_DIM), lambda b, g: (b, 0, g)),
                   pl.BlockSpec((None, lc, GROUP_DIM), lambda b, g: (b, 0, g))],
        out_shape=[jax.ShapeDtypeStruct((b, l, D_INNER), BF16), jax.ShapeDtypeStruct((b, lc, D_INNER), BF16)],
        scratch_shapes=seq_scratch(l) + seq_scratch(lc) + [pltpu.VMEM((D_STATE, GROUP_DIM), F32)] * 2,
        compiler_params=_cparams(("arbitrary", "arbitrary")),
        name="ssd",
    )(xbc_l, xbc_l, xbc_l, z_l, dt_l, xbc_c, xbc_c, xbc_c, z_c, dt_c,
      conv_w, conv_w, conv_w, conv_b, conv_b, conv_b, dtb_rows, nega_rows, dskip_row, ng_row)


def _attn_kernel(q_ref, k_ref, v_ref, o_ref):
    k = k_ref[...]
    v = v_ref[...]
    for r in range(Q_PER_KV):
        sl = slice(r * ATT_HEAD_DIM, (r + 1) * ATT_HEAD_DIM)
        s = lax.dot_general(q_ref[:, sl], k, (((1,), (1,)), ((), ())), preferred_element_type=F32)
        p = jnp.exp(s - jnp.max(s, axis=-1, keepdims=True))
        den = jnp.sum(p, axis=-1, keepdims=True)
        o = jnp.dot(p.astype(BF16), v, preferred_element_type=F32)
        o_ref[:, sl] = (o / den).astype(BF16)


def attention(q, k_all, v_all, tq):
    b, l, _ = q.shape
    nk = k_all.shape[1]
    width = Q_PER_KV * ATT_HEAD_DIM
    return pl.pallas_call(
        _attn_kernel,
        grid=(b, N_KV_HEADS, l // tq),
        in_specs=[pl.BlockSpec((None, tq, width), lambda b, h, i: (b, i, h)),
                  pl.BlockSpec((None, nk, ATT_HEAD_DIM), lambda b, h, i: (b, 0, h)),
                  pl.BlockSpec((None, nk, ATT_HEAD_DIM), lambda b, h, i: (b, 0, h))],
        out_specs=pl.BlockSpec((None, tq, width), lambda b, h, i: (b, i, h)),
        out_shape=jax.ShapeDtypeStruct((b, l, ATT_DIM), BF16),
        compiler_params=_cparams(("arbitrary", "arbitrary", "arbitrary")),
        name="attention",
    )(q, k_all, v_all)


def _merge_kernel(yn_ref, att_ref, gs_ref, ga_ref, x_ref, g2_ref, wso_ref, wao_ref, wo_ref, o_ref):
    y_ssd = jnp.dot(yn_ref[...], wso_ref[...], preferred_element_type=F32)
    y_att = jnp.dot(att_ref[...], wao_ref[...], preferred_element_type=F32)
    m = jax.nn.sigmoid(gs_ref[...].astype(F32)) * y_ssd + jax.nn.sigmoid(ga_ref[...].astype(F32)) * y_att
    u = jnp.dot(m.astype(BF16), wo_ref[...], preferred_element_type=F32)
    o_ref[...] = x_ref[...] + g2_ref[...] * u


def merge(yn, att, gs, ga, x, mods, row_of_b, wso, wao, wo, tm):
    b, l, d = x.shape
    tok = lambda b, i: (b, i, 0)
    const = lambda b, i: (0, 0)
    return pl.pallas_call(
        _merge_kernel,
        grid=(b, l // tm),
        in_specs=[pl.BlockSpec((None, tm, D_INNER), tok), pl.BlockSpec((None, tm, ATT_DIM), tok),
                  pl.BlockSpec((None, tm, d), tok), pl.BlockSpec((None, tm, d), tok),
                  pl.BlockSpec((None, tm, d), tok), _mod_spec(row_of_b, 5),
                  pl.BlockSpec((D_INNER, d), const), pl.BlockSpec((ATT_DIM, d), const),
                  pl.BlockSpec((d, d), const)],
        out_specs=pl.BlockSpec((None, tm, d), tok),
        out_shape=jax.ShapeDtypeStruct(x.shape, F32),
        compiler_params=_cparams(("arbitrary", "arbitrary")),
        name="merge",
    )(yn, att, gs, ga, x, mods, wso, wao, wo)


def _rope_tables(n_tok):
    rows = n_tok // GRID_W
    row = jnp.broadcast_to(jnp.arange(rows)[:, None], (rows, GRID_W)).reshape(-1).astype(F32)
    col = jnp.broadcast_to(jnp.arange(GRID_W)[None, :], (rows, GRID_W)).reshape(-1).astype(F32)
    inv_freq = ROPE_THETA ** (-(jnp.arange(0, ROPE_AXIS_DIM, 2, dtype=F32) / ROPE_AXIS_DIM))
    ar = row[:, None] * inv_freq
    ac = col[:, None] * inv_freq
    cos_t = jnp.concatenate([jnp.cos(ar), jnp.cos(ar), jnp.cos(ac), jnp.cos(ac)], axis=1)
    sin_t = jnp.concatenate([-jnp.sin(ar), jnp.sin(ar), -jnp.sin(ac), jnp.sin(ac)], axis=1)
    return cos_t, sin_t


def _permute_w_in(w_in):
    depth, d, _ = w_in.shape
    o = D_INNER + CONV_DIM
    dtf = w_in[..., o:o + SSD_HEADS].reshape(depth, d, SSD_GROUPS, HEADS_PER_GROUP)
    dtb = w_in[..., o + SSD_HEADS:o + 2 * SSD_HEADS].reshape(depth, d, SSD_GROUPS, HEADS_PER_GROUP)
    pad = jnp.zeros((depth, d, SSD_GROUPS, LANES - 2 * HEADS_PER_GROUP), w_in.dtype)
    dt_blk = jnp.concatenate([dtf, dtb, pad], axis=-1).reshape(depth, d, DT_COLS)
    rest = w_in[..., o + 2 * SSD_HEADS:]
    return jnp.concatenate([w_in[..., :o], rest, dt_blk], axis=-1).astype(BF16)


def _group_rows(v):
    depth = v.shape[0]
    f = v[:, 0].reshape(depth, SSD_GROUPS, HEADS_PER_GROUP)
    bk = v[:, 1].reshape(depth, SSD_GROUPS, HEADS_PER_GROUP)
    pad = jnp.zeros((depth, SSD_GROUPS, LANES - 2 * HEADS_PER_GROUP), v.dtype)
    return jnp.concatenate([f, bk, pad], axis=-1).reshape(depth, SSD_GROUPS, 1, LANES)


def kernel(x, c, ctx, c_ctx, w_ada, b_ada, norm_g, ffn1_w13, ffn1_w2, w_in, conv_w, conv_b,
           a_log, dt_bias, d_skip, ssd_norm_g, w_ssd_out, qk_norm_g, w_attn_out, w_out,
           ffn2_w13, ffn2_w2):
    batch, seq, d = x.shape
    lc = ctx.shape[1]
    assert batch + 1 <= MOD_ROWS and seq % 512 == 0 and lc % CHUNK == 0

    cond = jnp.zeros((MOD_ROWS, d), F32).at[:batch].set(c).at[batch].set(c_ctx)
    mods_all = ada_all(cond, w_ada, b_ada).reshape(DEPTH, MOD_ROWS, 1, N_MOD * d)

    cos_l, sin_l = _rope_tables(seq)
    cos_c = jnp.ones((lc, ATT_HEAD_DIM), F32)
    sin_c = jnp.zeros((lc, ATT_HEAD_DIM), F32)

    w_in_p = _permute_w_in(w_in)
    f1_w13, f1_w2 = ffn1_w13.astype(BF16), ffn1_w2.astype(BF16)
    f2_w13, f2_w2 = ffn2_w13.astype(BF16), ffn2_w2.astype(BF16)
    wso, wao, wo = w_ssd_out.astype(BF16), w_attn_out.astype(BF16), w_out.astype(BF16)
    dtb_rows = _group_rows(dt_bias)
    nega_rows = _group_rows(-jnp.exp(a_log.astype(F32)))
    dskip_rows = jnp.repeat(d_skip, SSD_HEAD_DIM, axis=-1).reshape(DEPTH, 1, D_INNER)

    lat_row = lambda b: b
    ctx_row = lambda b: batch
    tm_l, tm_c = 512, lc
    xc = ctx
    for i in range(DEPTH):
        last = i == DEPTH - 1
        mods = mods_all[i]
        ng = norm_g[i]
        x = ffn(x, mods, lat_row, 0, ng[0:1], f1_w13[i], f1_w2[i], tm_l)
        xc = ffn(xc, mods, ctx_row, 0, ng[0:1], f1_w13[i], f1_w2[i], tm_c)

        z, xbc, q, k, v, gs, ga, dt = inproj(x, mods, lat_row, ng[1:2], w_in_p[i], qk_norm_g[i], cos_l, sin_l, 256)
        zc, xbcc, qc, kc, vc, gsc, gac, dtc = inproj(xc, mods, ctx_row, ng[1:2], w_in_p[i], qk_norm_g[i],
                                                     cos_c, sin_c, tm_c)
        yn, ync = ssd(xbc, z, dt, xbcc, zc, dtc, conv_w[i], conv_b[i].reshape(1, CONV_DIM), dtb_rows[i],
                      nega_rows[i], dskip_rows[i], ssd_norm_g[i].reshape(1, D_INNER), not last)
        att = attention(q, jnp.concatenate([kc, k], axis=1), jnp.concatenate([vc, v], axis=1), 256)
        x = merge(yn, att, gs, ga, x, mods, lat_row, wso[i], wao[i], wo[i], tm_l)
        x = ffn(x, mods, lat_row, 6, ng[2:3], f2_w13[i], f2_w2[i], tm_l)
        if not last:
            attc = attention(qc, kc, vc, lc)
            xc = merge(ync, attc, gsc, gac, xc, mods, ctx_row, wso[i], wao[i], wo[i], tm_c)
            xc = ffn(xc, mods, ctx_row, 6, ng[2:3], f2_w13[i], f2_w2[i], tm_c)
    return x
```

```python
import functools
import math

import jax
import jax.numpy as jnp
from jax import lax
from jax.experimental import pallas as pl
from jax.experimental.pallas import tpu as pltpu

F32 = jnp.float32
BF16 = jnp.bfloat16

D_MODEL = 1024
DEPTH = 4
GRID_W = 64
EPS = 1e-6
N_MOD = 9
D_FF = 2816

D_INNER = 2048
SSD_HEAD_DIM = 64
SSD_HEADS = 32
SSD_GROUPS = 4
HEADS_PER_GROUP = SSD_HEADS // SSD_GROUPS
GROUP_DIM = HEADS_PER_GROUP * SSD_HEAD_DIM
D_STATE = 128
BC_DIM = SSD_GROUPS * D_STATE
CONV_DIM = D_INNER + 2 * BC_DIM
CHUNK = 128

ATT_HEAD_DIM = 128
N_Q_HEADS = 8
N_KV_HEADS = 2
Q_PER_KV = N_Q_HEADS // N_KV_HEADS
ATT_DIM = N_Q_HEADS * ATT_HEAD_DIM
KV_DIM = N_KV_HEADS * ATT_HEAD_DIM
ATT_SCALE = ATT_HEAD_DIM ** -0.5
ROPE_THETA = 10000.0
ROPE_AXIS_DIM = ATT_HEAD_DIM // 2

LANES = 128
PIECE = 2 * SSD_HEADS
LOG2E = math.log2(math.e)
OFF_Z = 0
OFF_XBC = OFF_Z + D_INNER
OFF_Q = OFF_XBC + CONV_DIM
OFF_K = OFF_Q + ATT_DIM
OFF_V = OFF_K + KV_DIM
OFF_GS = OFF_V + KV_DIM
OFF_GA = OFF_GS + D_MODEL
OFF_DT = OFF_GA + D_MODEL
IN_COLS = OFF_DT + LANES

MOD_ROWS = 16
VMEM_LIMIT = 56 * 1024 * 1024


def _cparams(sem):
    return pltpu.CompilerParams(dimension_semantics=sem, vmem_limit_bytes=VMEM_LIMIT)


def _silu(v):
    return v * jax.nn.sigmoid(v)


def _norm_mod(x, ng, sc, sh):
    r = lax.rsqrt(jnp.mean(x * x, axis=-1, keepdims=True) + EPS)
    return (x * r * ng) * (1.0 + sc) + sh


def _mm(a, b):
    return jnp.dot(a, b, preferred_element_type=F32)


def _ada_kernel(cond_ref, w_ref, b_ref, o_ref):
    a = _silu(cond_ref[...])
    o_ref[...] = jnp.dot(a, w_ref[...], preferred_element_type=F32,
                         precision=lax.Precision.HIGHEST) + b_ref[...]


def ada_all(cond, w_ada, b_ada, tn=1024):
    depth, d, n = w_ada.shape
    return pl.pallas_call(
        _ada_kernel,
        grid=(depth, n // tn),
        in_specs=[pl.BlockSpec((MOD_ROWS, d), lambda l, j: (0, 0)),
                  pl.BlockSpec((None, d, tn), lambda l, j: (l, 0, j)),
                  pl.BlockSpec((None, 1, tn), lambda l, j: (l, 0, j))],
        out_specs=pl.BlockSpec((None, MOD_ROWS, tn), lambda l, j: (l, 0, j)),
        out_shape=jax.ShapeDtypeStruct((depth, MOD_ROWS, n), F32),
        compiler_params=_cparams(("arbitrary", "arbitrary")),
        name="ada_params",
    )(cond, w_ada, b_ada.reshape(depth, 1, n))


def _mod_spec(row_of_b, k):
    return pl.BlockSpec((None, 1, D_MODEL), lambda b, i, *_: (row_of_b(b), 0, k))


def _ffn_kernel(x_ref, sh_ref, sc_ref, gt_ref, ng_ref, w1_ref, w3_ref, w2_ref, o_ref, h_scr, acc_scr):
    j = pl.program_id(2)

    @pl.when(j == 0)
    def _():
        h_scr[...] = _norm_mod(x_ref[...], ng_ref[...], sc_ref[...], sh_ref[...]).astype(BF16)

    h = h_scr[...]
    act = (_silu(_mm(h, w1_ref[...])) * _mm(h, w3_ref[...])).astype(BF16)
    part = _mm(act, w2_ref[...])

    @pl.when(j == 0)
    def _():
        acc_scr[...] = part

    @pl.when(j > 0)
    def _():
        acc_scr[...] += part

    @pl.when(j == pl.num_programs(2) - 1)
    def _():
        o_ref[...] = x_ref[...] + 0.5 * gt_ref[...] * acc_scr[...]


def ffn(x, mods, row_of_b, k0, ng, w13, w2, tm, tf=D_FF // 2):
    b, l, d = x.shape
    nj = D_FF // tf
    return pl.pallas_call(
        _ffn_kernel,
        grid=(b, l // tm, nj),
        in_specs=[pl.BlockSpec((None, tm, d), lambda b, i, j: (b, i, 0)),
                  _mod_spec(row_of_b, k0), _mod_spec(row_of_b, k0 + 1), _mod_spec(row_of_b, k0 + 2),
                  pl.BlockSpec((1, d), lambda b, i, j: (0, 0)),
                  pl.BlockSpec((d, tf), lambda b, i, j: (0, j)),
                  pl.BlockSpec((d, tf), lambda b, i, j: (0, nj + j)),
                  pl.BlockSpec((tf, d), lambda b, i, j: (j, 0))],
        out_specs=pl.BlockSpec((None, tm, d), lambda b, i, j: (b, i, 0)),
        out_shape=jax.ShapeDtypeStruct(x.shape, F32),
        scratch_shapes=[pltpu.VMEM((tm, d), BF16), pltpu.VMEM((tm, d), F32)],
        compiler_params=_cparams(("arbitrary", "arbitrary", "arbitrary")),
        name="ffn",
    )(x, mods, mods, mods, ng, w13, w13, w2)


def _swap32(v):
    lane = lax.broadcasted_iota(jnp.int32, v.shape, 1)
    return jnp.where((lane & 32) == 0, pltpu.roll(v, LANES - 32, axis=1), pltpu.roll(v, 32, axis=1))


def _inproj_kernel(x_ref, sh_ref, sc_ref, ng_ref, w_ref, qkg_ref, cos_ref, sin_ref,
                   z_ref, xbc_ref, q_ref, k_ref, v_ref, gs_ref, ga_ref, dt_ref):
    h = _norm_mod(x_ref[...], ng_ref[...], sc_ref[...], sh_ref[...]).astype(BF16)

    def proj(lo, hi):
        return _mm(h, w_ref[:, lo:hi])

    def proj_to(o_ref, lo, width, step=1024):
        for c in range(0, width, step):
            o_ref[:, c:c + step] = proj(lo + c, lo + c + step).astype(BF16)

    proj_to(z_ref, OFF_Z, D_INNER)
    proj_to(xbc_ref, OFF_XBC, CONV_DIM)
    proj_to(v_ref, OFF_V, KV_DIM, KV_DIM)
    proj_to(gs_ref, OFF_GS, D_MODEL)
    proj_to(ga_ref, OFF_GA, D_MODEL)
    dt_ref[...] = proj(OFF_DT, IN_COLS)

    cos = cos_ref[...]
    sin = sin_ref[...]

    def head(col, gain, scale):
        t = proj(col, col + ATT_HEAD_DIM)
        t = t * lax.rsqrt(jnp.mean(t * t, axis=-1, keepdims=True) + EPS) * gain
        t = t * cos + _swap32(t) * sin
        return (t * scale).astype(BF16) if scale != 1.0 else t.astype(BF16)

    q_scale = ATT_SCALE * math.log2(math.e)
    for hh in range(N_Q_HEADS):
        q_ref[:, hh * ATT_HEAD_DIM:(hh + 1) * ATT_HEAD_DIM] = head(OFF_Q + hh * ATT_HEAD_DIM, qkg_ref[0:1, :], q_scale)
    for hh in range(N_KV_HEADS):
        k_ref[:, hh * ATT_HEAD_DIM:(hh + 1) * ATT_HEAD_DIM] = head(OFF_K + hh * ATT_HEAD_DIM, qkg_ref[1:2, :], 1.0)


def inproj(x, mods, row_of_b, ng, w_in, qk_g, cos_t, sin_t, tm):
    b, l, d = x.shape
    tok = lambda b, i: (b, i, 0)
    const = lambda b, i: (0, 0)
    widths = (D_INNER, CONV_DIM, ATT_DIM, KV_DIM, KV_DIM, D_MODEL, D_MODEL, LANES)
    dtypes = (BF16, BF16, BF16, BF16, BF16, BF16, BF16, F32)
    return pl.pallas_call(
        _inproj_kernel,
        grid=(b, l // tm),
        in_specs=[pl.BlockSpec((None, tm, d), tok),
                  _mod_spec(row_of_b, 3), _mod_spec(row_of_b, 4),
                  pl.BlockSpec((1, d), const),
                  pl.BlockSpec((d, IN_COLS), const, pipeline_mode=pl.Buffered(1)),
                  pl.BlockSpec((2, ATT_HEAD_DIM), const),
                  pl.BlockSpec((tm, ATT_HEAD_DIM), lambda b, i: (i, 0)),
                  pl.BlockSpec((tm, ATT_HEAD_DIM), lambda b, i: (i, 0))],
        out_specs=[pl.BlockSpec((None, tm, w), tok) for w in widths],
        out_shape=[jax.ShapeDtypeStruct((b, l, w), dt) for w, dt in zip(widths, dtypes)],
        compiler_params=_cparams(("arbitrary", "arbitrary")),
        name="inproj",
    )(x, mods, mods, ng, w_in, qk_g, cos_t, sin_t)


def _scan_rows(v, suffix):
    n = v.shape[0]
    row = lax.broadcasted_iota(jnp.int32, v.shape, 0)
    k = 1
    while k < n:
        if suffix:
            v = v + jnp.where(row < n - k, pltpu.roll(v, n - k, axis=0), 0.0)
        else:
            v = v + jnp.where(row >= k, pltpu.roll(v, k, axis=0), 0.0)
        k *= 2
    return v


def _split_bf16(v, low, pieces):
    hi = v.astype(BF16).astype(F32)
    r1 = v - hi
    mid = r1.astype(BF16).astype(F32)
    first = jnp.where(low, hi, mid).astype(BF16)
    if pieces == 2:
        return first
    return jnp.concatenate([first, jnp.where(low, r1 - mid, 0.0).astype(BF16)], axis=1)


def _dtprep_kernel(raw_ref, bias_ref, nega_ref, cs_ref, e_ref, w_ref, rt_ref):
    n_chunks = raw_ref.shape[0] // CHUNK
    lane = lax.broadcasted_iota(jnp.int32, (CHUNK, LANES), 1)
    is_bwd = (lane & HEADS_PER_GROUP) != 0
    low = lane < PIECE
    for c in range(n_chunks):
        rows = slice(c * CHUNK, (c + 1) * CHUNK)
        v = raw_ref[rows, :] + bias_ref[...]
        dt = jnp.maximum(v, 0.0) + jnp.log1p(jnp.exp(-jnp.abs(v)))
        da = dt * nega_ref[...]
        pre = _scan_rows(da, suffix=False)
        cs = jnp.where(is_bwd, _scan_rows(da, suffix=True), pre)
        tot = pre[CHUNK - 1:CHUNK, :]
        cs_ref[rows, :] = _split_bf16(cs * LOG2E, low, 3)
        e_ref[rows, :] = _split_bf16(jnp.exp(cs), low, 2)
        w_ref[rows, :] = _split_bf16(dt * jnp.exp(tot - cs), low, 2)
        rt_ref[c] = ((cs - jnp.log(dt)) * LOG2E).T[0:PIECE, :]


def dtprep(raw, bias_row, nega_row):
    b, l, _ = raw.shape
    r = min(l, 1024)
    tok = lambda b, i: (b, i, 0)
    const = lambda b, i: (0, 0)
    return pl.pallas_call(
        _dtprep_kernel,
        grid=(b, l // r),
        in_specs=[pl.BlockSpec((None, r, LANES), tok), pl.BlockSpec((1, LANES), const),
                  pl.BlockSpec((1, LANES), const)],
        out_specs=[pl.BlockSpec((None, r, 2 * LANES), tok), pl.BlockSpec((None, r, LANES), tok),
                   pl.BlockSpec((None, r, LANES), tok),
                   pl.BlockSpec((None, r // CHUNK, PIECE, CHUNK), lambda b, i: (b, i, 0, 0))],
        out_shape=[jax.ShapeDtypeStruct((b, l, 2 * LANES), BF16), jax.ShapeDtypeStruct((b, l, LANES), BF16),
                   jax.ShapeDtypeStruct((b, l, LANES), BF16),
                   jax.ShapeDtypeStruct((b, l // CHUNK, PIECE, CHUNK), F32)],
        compiler_params=_cparams(("arbitrary", "arbitrary")),
        name="dtprep",
    )(raw, bias_row, nega_row)


def _ssd_kernel(with_ctx_out,
                xs_l, bm_l, cm_l, cs_l, e_l, w_l, rt_l, xs_c, bm_c, cm_c, cs_c, e_c, w_c, rt_c,
                cwx, cwb, cwc, cbx, cbb, cbc, dskip, selcs, expf, expb,
                y_l, y_c,
                xc_l, bt_l, cc_l, sb_l, xc_c, bt_c, cc_c, sb_c, st_f, st_b):
    lat = dict(xs=xs_l, bm=bm_l, cm=cm_l, cs=cs_l, e=e_l, w=w_l, rt=rt_l, y=y_l,
               xc=xc_l, bt=bt_l, cc=cc_l, sb=sb_l, out=True)
    ctx = dict(xs=xs_c, bm=bm_c, cm=cm_c, cs=cs_c, e=e_c, w=w_c, rt=rt_c, y=y_c,
               xc=xc_c, bt=bt_c, cc=cc_c, sb=sb_c, out=with_ctx_out)

    def rows(c):
        return pl.ds(pl.multiple_of(c * CHUNK, CHUNK), CHUNK)

    def bwd_pass(s):
        n_chunks = s["xs"].shape[0] // CHUNK
        row = lax.broadcasted_iota(jnp.int32, (CHUNK, 1), 0)

        def conv(src, w_ref, b_ref, c):
            cur = src[rows(c), :].astype(F32)
            p_at = pl.multiple_of(jnp.maximum(c * CHUNK - 16, 0), 16)
            n_at = pl.multiple_of(jnp.minimum((c + 1) * CHUNK, n_chunks * CHUNK - 16), 16)
            prev = src[pl.ds(p_at, 16), :].astype(F32)[15:16, :] * (c > 0).astype(F32)
            nxt = src[pl.ds(n_at, 16), :].astype(F32)[0:1, :] * (c < n_chunks - 1).astype(F32)
            up = jnp.where(row == 0, prev, pltpu.roll(cur, 1, axis=0))
            dn = jnp.where(row == CHUNK - 1, nxt, pltpu.roll(cur, CHUNK - 1, axis=0))
            return _silu(w_ref[0:1, :] * up + w_ref[1:2, :] * cur + w_ref[2:3, :] * dn + b_ref[...])

        def body(i, carry):
            c = n_chunks - 1 - i
            xcv = conv(s["xs"], cwx, cbx, c)
            bt = conv(s["bm"], cwb, cbb, c).T.astype(BF16)
            s["xc"][rows(c), :] = xcv.astype(BF16)
            s["cc"][rows(c), :] = conv(s["cm"], cwc, cbc, c).astype(BF16)
            s["bt"][c] = bt
            s["sb"][c] = st_b[...].astype(BF16)
            w_b = _mm(s["w"][rows(c), :], expb[...])
            dec = _mm(s["e"][pl.ds(pl.multiple_of(c * CHUNK, CHUNK), 16), :], expb[...])[0:1, :]
            st_b[...] = st_b[...] * dec + _mm(bt, (xcv * w_b).astype(BF16))
            return carry

        lax.fori_loop(0, n_chunks, body, 0, unroll=2)

    def fwd_pass(s):
        n_chunks = s["xs"].shape[0] // CHUNK
        ri = lax.broadcasted_iota(jnp.int32, (CHUNK, CHUNK), 0)
        ci = lax.broadcasted_iota(jnp.int32, (CHUNK, CHUNK), 1)
        lower = ri >= ci
        upper = ri <= ci
        first = lax.broadcasted_iota(jnp.int32, (CHUNK, LANES), 1) < SSD_HEAD_DIM

        def body(c, carry):
            xc = s["xc"][rows(c), :]
            bt = s["bt"][c]
            if s["out"]:
                cc = s["cc"][rows(c), :]
                colb = _mm(s["cs"][rows(c), :], selcs[...])
                rt = s["rt"][c]
                cb = _mm(cc, bt)
                y_f = _mm(cc, st_f[...].astype(BF16))
                y_b = _mm(cc, s["sb"][c])
                blocks, dec = [], []
                for j in range(HEADS_PER_GROUP // 2):
                    col =[colb[:, (4 * j + t) * CHUNK:(4 * j + t + 1) * CHUNK] for t in range(4)]
                    ms = []
                    for t, hh in enumerate((2 * j, 2 * j + 1)):
                        hb = HEADS_PER_GROUP + hh
                        d_f = jnp.exp2(jnp.where(lower, col[t] - rt[hh:hh + 1, :], -jnp.inf))
                        d_b = jnp.exp2(jnp.where(upper, col[2 + t] - rt[hb:hb + 1, :], -jnp.inf))
                        ms.append((cb * (d_f + d_b)).astype(BF16))
                    xblk = xc[:, j * LANES:(j + 1) * LANES]
                    zero = jnp.zeros_like(xblk)
                    x_bd = jnp.concatenate([jnp.where(first, xblk, zero), jnp.where(first, zero, xblk)], axis=0)
                    yd = _mm(jnp.concatenate(ms, axis=1), x_bd)
                    e_f = jnp.exp2(jnp.where(first, col[0], col[1]))
                    e_b = jnp.exp2(jnp.where(first, col[2], col[3]))
                    sl = slice(j * LANES, (j + 1) * LANES)
                    blocks.append(yd + y_f[:, sl] * e_f + y_b[:, sl] * e_b + xblk.astype(F32) * dskip[:, sl])
                    dec.append(e_f[CHUNK - 1:CHUNK, :])
                s["y"][rows(c), :] = jnp.concatenate(blocks, axis=1).astype(BF16)
                dec = jnp.concatenate(dec, axis=1)
            else:
                last16 = pl.ds(pl.multiple_of(c * CHUNK + CHUNK - 16, 16), 16)
                dec = _mm(s["e"][last16, :], expf[...])[15:16, :]
            xw = (xc.astype(F32) * _mm(s["w"][rows(c), :], expf[...])).astype(BF16)
            st_f[...] = st_f[...] * dec + _mm(bt, xw)
            return carry

        lax.fori_loop(0, n_chunks, body, 0, unroll=2)

    st_b[...] = jnp.zeros_like(st_b)
    bwd_pass(ctx)
    bwd_pass(lat)
    st_f[...] = jnp.zeros_like(st_f)
    fwd_pass(ctx)
    fwd_pass(lat)
    if not with_ctx_out:
        y_c[...] = jnp.zeros_like(y_c)


def _head_select_tables():
    g = jnp.arange(SSD_GROUPS)[:, None, None]
    k = jnp.arange(2 * LANES)[None, :, None]
    n = jnp.arange(2 * HEADS_PER_GROUP * CHUNK)[None, None, :]
    q = n // CHUNK
    head_dir = (q % 4 // 2) * HEADS_PER_GROUP + 2 * (q // 4) + q % 2
    selcs = ((k % PIECE) == 2 * HEADS_PER_GROUP * g + head_dir) & (k < 3 * PIECE)
    k2 = jnp.arange(LANES)[None, :, None]
    n2 = jnp.arange(GROUP_DIM)[None, None, :]
    expf = (k2 % PIECE) == 2 * HEADS_PER_GROUP * g + n2 // SSD_HEAD_DIM
    expb = (k2 % PIECE) == 2 * HEADS_PER_GROUP * g + HEADS_PER_GROUP + n2 // SSD_HEAD_DIM
    return selcs.astype(BF16), expf.astype(BF16), expb.astype(BF16)


def ssd(xbc_l, tabs_l, xbc_c, tabs_c, conv_w, conv_b, dskip_row, tables, with_ctx_out):
    b, l, _ = xbc_l.shape
    lc = xbc_c.shape[1]
    n_bg = D_INNER // D_STATE

    def seq_specs(n):
        return [pl.BlockSpec((None, n, GROUP_DIM), lambda b, g: (b, 0, g)),
                pl.BlockSpec((None, n, D_STATE), lambda b, g: (b, 0, n_bg + g)),
                pl.BlockSpec((None, n, D_STATE), lambda b, g: (b, 0, n_bg + SSD_GROUPS + g)),
                pl.BlockSpec((None, n, 2 * LANES), lambda b, g: (b, 0, 0)),
                pl.BlockSpec((None, n, LANES), lambda b, g: (b, 0, 0)),
                pl.BlockSpec((None, n, LANES), lambda b, g: (b, 0, 0)),
                pl.BlockSpec((None, n // CHUNK, 2 * HEADS_PER_GROUP, CHUNK), lambda b, g: (b, 0, g, 0))]

    def seq_scratch(n):
        return [pltpu.VMEM((n, GROUP_DIM), BF16), pltpu.VMEM((n // CHUNK, D_STATE, CHUNK), BF16),
                pltpu.VMEM((n, D_STATE), BF16), pltpu.VMEM((n // CHUNK, D_STATE, GROUP_DIM), BF16)]

    par_specs = [pl.BlockSpec((3, GROUP_DIM), lambda b, g: (0, g)),
                 pl.BlockSpec((3, D_STATE), lambda b, g: (0, n_bg + g)),
                 pl.BlockSpec((3, D_STATE), lambda b, g: (0, n_bg + SSD_GROUPS + g)),
                 pl.BlockSpec((1, GROUP_DIM), lambda b, g: (0, g)),
                 pl.BlockSpec((1, D_STATE), lambda b, g: (0, n_bg + g)),
                 pl.BlockSpec((1, D_STATE), lambda b, g: (0, n_bg + SSD_GROUPS + g)),
                 pl.BlockSpec((1, GROUP_DIM), lambda b, g: (0, g)),
                 pl.BlockSpec((None, 2 * LANES, 2 * HEADS_PER_GROUP * CHUNK), lambda b, g: (g, 0, 0)),
                 pl.BlockSpec((None, LANES, GROUP_DIM), lambda b, g: (g, 0, 0)),
                 pl.BlockSpec((None, LANES, GROUP_DIM), lambda b, g: (g, 0, 0))]
    return pl.pallas_call(
        functools.partial(_ssd_kernel, with_ctx_out),
        grid=(b, SSD_GROUPS),
        in_specs=seq_specs(l) + seq_specs(lc) + par_specs,
        out_specs=[pl.BlockSpec((None, l, GROUP_DIM), lambda b, g: (b, 0, g)),
                   pl.BlockSpec((None, lc, GROUP_DIM), lambda b, g: (b, 0, g))],
        out_shape=[jax.ShapeDtypeStruct((b, l, D_INNER), BF16), jax.ShapeDtypeStruct((b, lc, D_INNER), BF16)],
        scratch_shapes=seq_scratch(l) + seq_scratch(lc) + [pltpu.VMEM((D_STATE, GROUP_DIM), F32)] * 2,
        compiler_params=_cparams(("arbitrary", "arbitrary")),
        name="ssd",
    )(xbc_l, xbc_l, xbc_l, *tabs_l, xbc_c, xbc_c, xbc_c, *tabs_c,
      conv_w, conv_w, conv_w, conv_b, conv_b, conv_b, dskip_row, *tables)


def _attn_kernel(q_ref, k_ref, v_ref, o_ref):
    k = k_ref[...]
    v = v_ref[...]
    for r in range(Q_PER_KV):
        sl = slice(r * ATT_HEAD_DIM, (r + 1) * ATT_HEAD_DIM)
        s = lax.dot_general(q_ref[:, sl], k, (((1,), (1,)), ((), ())), preferred_element_type=F32)
        p = jnp.exp2(s - jnp.max(s, axis=-1, keepdims=True))
        den = jnp.sum(p, axis=-1, keepdims=True)
        o = _mm(p.astype(BF16), v)
        o_ref[:, sl] = (o / den).astype(BF16)


def attention(q, k_all, v_all, tq):
    b, l, _ = q.shape
    nk = k_all.shape[1]
    width = Q_PER_KV * ATT_HEAD_DIM
    return pl.pallas_call(
        _attn_kernel,
        grid=(b, N_KV_HEADS, l // tq),
        in_specs=[pl.BlockSpec((None, tq, width), lambda b, h, i: (b, i, h)),
                  pl.BlockSpec((None, nk, ATT_HEAD_DIM), lambda b, h, i: (b, 0, h)),
                  pl.BlockSpec((None, nk, ATT_HEAD_DIM), lambda b, h, i: (b, 0, h))],
        out_specs=pl.BlockSpec((None, tq, width), lambda b, h, i: (b, i, h)),
        out_shape=jax.ShapeDtypeStruct((b, l, ATT_DIM), BF16),
        compiler_params=_cparams(("arbitrary", "arbitrary", "arbitrary")),
        name="attention",
    )(q, k_all, v_all)


def _merge_kernel(y_ref, z_ref, sng_ref, att_ref, gs_ref, ga_ref, x_ref, g2_ref, wso_ref, wao_ref, wo_ref, o_ref):
    y_ssd = None
    for g in range(SSD_GROUPS):
        sl = slice(g * GROUP_DIM, (g + 1) * GROUP_DIM)
        yg = y_ref[:, sl].astype(F32) * _silu(z_ref[:, sl].astype(F32))
        yn = yg * lax.rsqrt(jnp.mean(yg * yg, axis=-1, keepdims=True) + EPS) * sng_ref[:, sl]
        part = _mm(yn.astype(BF16), wso_ref[sl, :])
        y_ssd = part if y_ssd is None else y_ssd + part
    y_att = _mm(att_ref[...], wao_ref[...])
    m = jax.nn.sigmoid(gs_ref[...].astype(F32)) * y_ssd + jax.nn.sigmoid(ga_ref[...].astype(F32)) * y_att
    u = _mm(m.astype(BF16), wo_ref[...])
    o_ref[...] = x_ref[...] + g2_ref[...] * u


def merge(y, z, sng, att, gs, ga, x, mods, row_of_b, wso, wao, wo, tm):
    b, l, d = x.shape
    tok = lambda b, i: (b, i, 0)
    const = lambda b, i: (0, 0)
    return pl.pallas_call(
        _merge_kernel,
        grid=(b, l // tm),
        in_specs=[pl.BlockSpec((None, tm, D_INNER), tok), pl.BlockSpec((None, tm, D_INNER), tok),
                  pl.BlockSpec((1, D_INNER), const), pl.BlockSpec((None, tm, ATT_DIM), tok),
                  pl.BlockSpec((None, tm, d), tok), pl.BlockSpec((None, tm, d), tok),
                  pl.BlockSpec((None, tm, d), tok), _mod_spec(row_of_b, 5),
                  pl.BlockSpec((D_INNER, d), const), pl.BlockSpec((ATT_DIM, d), const),
                  pl.BlockSpec((d, d), const)],
        out_specs=pl.BlockSpec((None, tm, d), tok),
        out_shape=jax.ShapeDtypeStruct(x.shape, F32),
        compiler_params=_cparams(("arbitrary", "arbitrary")),
        name="merge",
    )(y, z, sng, att, gs, ga, x, mods, wso, wao, wo)


def _rope_tables(n_tok):
    rows = n_tok // GRID_W
    row = jnp.broadcast_to(jnp.arange(rows)[:, None], (rows, GRID_W)).reshape(-1).astype(F32)
    col = jnp.broadcast_to(jnp.arange(GRID_W)[None, :], (rows, GRID_W)).reshape(-1).astype(F32)
    inv_freq = ROPE_THETA ** (-(jnp.arange(0, ROPE_AXIS_DIM, 2, dtype=F32) / ROPE_AXIS_DIM))
    ar = row[:, None] * inv_freq
    ac = col[:, None] * inv_freq
    cos_t = jnp.concatenate([jnp.cos(ar), jnp.cos(ar), jnp.cos(ac), jnp.cos(ac)], axis=1)
    sin_t = jnp.concatenate([-jnp.sin(ar), jnp.sin(ar), -jnp.sin(ac), jnp.sin(ac)], axis=1)
    return cos_t, sin_t


def _dt_lanes(fwd, bwd):
    lead = fwd.shape[:-1]
    f = fwd.reshape(*lead, SSD_GROUPS, 1, HEADS_PER_GROUP)
    bk = bwd.reshape(*lead, SSD_GROUPS, 1, HEADS_PER_GROUP)
    half = jnp.concatenate([f, bk], axis=-2).reshape(*lead, PIECE)
    return jnp.concatenate([half, half], axis=-1)


def _permute_w_in(w_in):
    o = D_INNER + CONV_DIM
    dt_cols = _dt_lanes(w_in[..., o:o + SSD_HEADS], w_in[..., o + SSD_HEADS:o + 2 * SSD_HEADS])
    return jnp.concatenate([w_in[..., :o], w_in[..., o + 2 * SSD_HEADS:], dt_cols], axis=-1).astype(BF16)


def kernel(x, c, ctx, c_ctx, w_ada, b_ada, norm_g, ffn1_w13, ffn1_w2, w_in, conv_w, conv_b,
           a_log, dt_bias, d_skip, ssd_norm_g, w_ssd_out, qk_norm_g, w_attn_out, w_out,
           ffn2_w13, ffn2_w2):
    batch, seq, d = x.shape
    lc = ctx.shape[1]
    assert batch + 1 <= MOD_ROWS and seq % 512 == 0 and lc % CHUNK == 0

    cond = jnp.zeros((MOD_ROWS, d), F32).at[:batch].set(c).at[batch].set(c_ctx)
    mods_all = ada_all(cond, w_ada, b_ada).reshape(DEPTH, MOD_ROWS, 1, N_MOD * d)

    cos_l, sin_l = _rope_tables(seq)
    cos_c = jnp.ones((lc, ATT_HEAD_DIM), F32)
    sin_c = jnp.zeros((lc, ATT_HEAD_DIM), F32)

    w_in_p = _permute_w_in(w_in)
    f1_w13, f1_w2 = ffn1_w13.astype(BF16), ffn1_w2.astype(BF16)
    f2_w13, f2_w2 = ffn2_w13.astype(BF16), ffn2_w2.astype(BF16)
    wso, wao, wo = w_ssd_out.astype(BF16), w_attn_out.astype(BF16), w_out.astype(BF16)
    dtb_rows = _dt_lanes(dt_bias[:, 0], dt_bias[:, 1]).reshape(DEPTH, 1, LANES)
    nega = -jnp.exp(a_log.astype(F32))
    nega_rows = _dt_lanes(nega[:, 0], nega[:, 1]).reshape(DEPTH, 1, LANES)
    dskip_rows = jnp.repeat(d_skip, SSD_HEAD_DIM, axis=-1).reshape(DEPTH, 1, D_INNER)
    tables = _head_select_tables()

    lat_row = lambda b: b
    ctx_row = lambda b: batch
    tm_l, tm_c = 512, lc
    xc = ctx
    for i in range(DEPTH):
        last = i == DEPTH - 1
        mods = mods_all[i]
        ng = norm_g[i]
        sng = ssd_norm_g[i].reshape(1, D_INNER)
        x = ffn(x, mods, lat_row, 0, ng[0:1], f1_w13[i], f1_w2[i], tm_l)
        xc = ffn(xc, mods, ctx_row, 0, ng[0:1], f1_w13[i], f1_w2[i], tm_c)

        z, xbc, q, k, v, gs, ga, dt = inproj(x, mods, lat_row, ng[1:2], w_in_p[i], qk_norm_g[i], cos_l, sin_l, tm_l)
        zc, xbcc, qc, kc, vc, gsc, gac, dtc = inproj(xc, mods, ctx_row, ng[1:2], w_in_p[i], qk_norm_g[i],
                                                     cos_c, sin_c, tm_c)
        tabs = dtprep(dt, dtb_rows[i], nega_rows[i])
        tabs_c = dtprep(dtc, dtb_rows[i], nega_rows[i])
        y, yc = ssd(xbc, tabs, xbcc, tabs_c, conv_w[i], conv_b[i].reshape(1, CONV_DIM), dskip_rows[i],
                    tables, not last)
        att = attention(q, jnp.concatenate([kc, k], axis=1), jnp.concatenate([vc, v], axis=1), 256)
        x = merge(y, z, sng, att, gs, ga, x, mods, lat_row, wso[i], wao[i], wo[i], tm_l)
        x = ffn(x, mods, lat_row, 6, ng[2:3], f2_w13[i], f2_w2[i], tm_l)
        if not last:
            attc = attention(qc, kc, vc, lc)
            xc = merge(yc, zc, sng, attc, gsc, gac, xc, mods, ctx_row, wso[i], wao[i], wo[i], tm_c)
            xc = ffn(xc, mods, ctx_row, 6, ng[2:3], f2_w13[i], f2_w2[i], tm_c)
    return x
```

```python
import functools
import math

import jax
import jax.numpy as jnp
from jax import lax
from jax.experimental import pallas as pl
from jax.experimental.pallas import tpu as pltpu

F32 = jnp.float32
BF16 = jnp.bfloat16

D_MODEL = 1024
DEPTH = 4
GRID_W = 64
EPS = 1e-6
N_MOD = 9
D_FF = 2816

D_INNER = 2048
SSD_HEAD_DIM = 64
SSD_HEADS = 32
SSD_GROUPS = 4
HEADS_PER_GROUP = SSD_HEADS // SSD_GROUPS
GROUP_DIM = HEADS_PER_GROUP * SSD_HEAD_DIM
D_STATE = 128
BC_DIM = SSD_GROUPS * D_STATE
CONV_DIM = D_INNER + 2 * BC_DIM
CHUNK = 128

ATT_HEAD_DIM = 128
N_Q_HEADS = 8
N_KV_HEADS = 2
Q_PER_KV = N_Q_HEADS // N_KV_HEADS
ATT_DIM = N_Q_HEADS * ATT_HEAD_DIM
KV_DIM = N_KV_HEADS * ATT_HEAD_DIM
ATT_SCALE = ATT_HEAD_DIM ** -0.5
ROPE_THETA = 10000.0
ROPE_AXIS_DIM = ATT_HEAD_DIM // 2

LANES = 128
PIECE = 2 * SSD_HEADS
LOG2E = math.log2(math.e)
OFF_Z = 0
OFF_XBC = OFF_Z + D_INNER
OFF_Q = OFF_XBC + CONV_DIM
OFF_K = OFF_Q + ATT_DIM
OFF_V = OFF_K + KV_DIM
OFF_GS = OFF_V + KV_DIM
OFF_GA = OFF_GS + D_MODEL
OFF_DT = OFF_GA + D_MODEL
IN_COLS = OFF_DT + LANES

MOD_ROWS = 16
VMEM_LIMIT = 56 * 1024 * 1024


def _cparams(sem):
    return pltpu.CompilerParams(dimension_semantics=sem, vmem_limit_bytes=VMEM_LIMIT)


def _silu(v):
    return v * jax.nn.sigmoid(v)


def _norm_mod(x, ng, sc, sh):
    r = lax.rsqrt(jnp.mean(x * x, axis=-1, keepdims=True) + EPS)
    return (x * r * ng) * (1.0 + sc) + sh


def _mm(a, b):
    return jnp.dot(a, b, preferred_element_type=F32)


def _ada_kernel(cond_ref, w_ref, b_ref, o_ref):
    a = _silu(cond_ref[...])
    o_ref[...] = jnp.dot(a, w_ref[...], preferred_element_type=F32,
                         precision=lax.Precision.HIGHEST) + b_ref[...]


def ada_all(cond, w_ada, b_ada, tn=1024):
    depth, d, n = w_ada.shape
    return pl.pallas_call(
        _ada_kernel,
        grid=(depth, n // tn),
        in_specs=[pl.BlockSpec((MOD_ROWS, d), lambda l, j: (0, 0)),
                  pl.BlockSpec((None, d, tn), lambda l, j: (l, 0, j)),
                  pl.BlockSpec((None, 1, tn), lambda l, j: (l, 0, j))],
        out_specs=pl.BlockSpec((None, MOD_ROWS, tn), lambda l, j: (l, 0, j)),
        out_shape=jax.ShapeDtypeStruct((depth, MOD_ROWS, n), F32),
        compiler_params=_cparams(("arbitrary", "arbitrary")),
        name="ada_params",
    )(cond, w_ada, b_ada.reshape(depth, 1, n))


def _mod_spec(row_of_b, k):
    return pl.BlockSpec((None, 1, D_MODEL), lambda b, i, *_: (row_of_b(b), 0, k))


FF_STEP = 2816


def _ffn_kernel(x_ref, sh_ref, sc_ref, gt_ref, ng_ref, w13_ref, w2_ref, o_ref):
    x = x_ref[...]
    h = _norm_mod(x, ng_ref[...], sc_ref[...], sh_ref[...]).astype(BF16)
    acc = None
    for c in range(0, D_FF, FF_STEP):
        a = _mm(h, w13_ref[:, c:c + FF_STEP])
        g = _mm(h, w13_ref[:, D_FF + c:D_FF + c + FF_STEP])
        part = _mm((_silu(a) * g).astype(BF16), w2_ref[c:c + FF_STEP, :])
        acc = part if acc is None else acc + part
    o_ref[...] = x + 0.5 * gt_ref[...] * acc


def ffn(x, mods, row_of_b, k0, ng, w13, w2, tm):
    b, l, d = x.shape
    tok = lambda b, i: (b, i, 0)
    const = lambda b, i: (0, 0)
    return pl.pallas_call(
        _ffn_kernel,
        grid=(b, l // tm),
        in_specs=[pl.BlockSpec((None, tm, d), tok),
                  _mod_spec(row_of_b, k0), _mod_spec(row_of_b, k0 + 1), _mod_spec(row_of_b, k0 + 2),
                  pl.BlockSpec((1, d), const),
                  pl.BlockSpec((d, 2 * D_FF), const, pipeline_mode=pl.Buffered(1)),
                  pl.BlockSpec((D_FF, d), const, pipeline_mode=pl.Buffered(1))],
        out_specs=pl.BlockSpec((None, tm, d), tok),
        out_shape=jax.ShapeDtypeStruct(x.shape, F32),
        compiler_params=_cparams(("arbitrary", "arbitrary")),
        name="ffn",
    )(x, mods, mods, mods, ng, w13, w2)


def _swap32(v):
    lane = lax.broadcasted_iota(jnp.int32, v.shape, 1)
    return jnp.where((lane & 32) == 0, pltpu.roll(v, LANES - 32, axis=1), pltpu.roll(v, 32, axis=1))


def _inproj_kernel(x_ref, sh_ref, sc_ref, ng_ref, w_ref, qkg_ref, cos_ref, sin_ref,
                   z_ref, xbc_ref, q_ref, k_ref, v_ref, gs_ref, ga_ref, dt_ref):
    h = _norm_mod(x_ref[...], ng_ref[...], sc_ref[...], sh_ref[...]).astype(BF16)

    def proj(lo, hi):
        return _mm(h, w_ref[:, lo:hi])

    def proj_to(o_ref, lo, width, step=1024):
        for c in range(0, width, step):
            o_ref[:, c:c + step] = proj(lo + c, lo + c + step).astype(BF16)

    cos = cos_ref[...]
    sin = sin_ref[...]

    def heads_to(o_ref, lo, n_heads, gain, scale):
        for h0 in range(0, n_heads, 2):
            pair = proj(lo + h0 * ATT_HEAD_DIM, lo + (h0 + 2) * ATT_HEAD_DIM)
            for hh in (h0, h0 + 1):
                t = pair[:, (hh - h0) * ATT_HEAD_DIM:(hh - h0 + 1) * ATT_HEAD_DIM]
                t = t * lax.rsqrt(jnp.mean(t * t, axis=-1, keepdims=True) + EPS) * gain
                t = t * cos + _swap32(t) * sin
                t = t * scale if scale != 1.0 else t
                o_ref[:, hh * ATT_HEAD_DIM:(hh + 1) * ATT_HEAD_DIM] = t.astype(BF16)

    heads_to(q_ref, OFF_Q, N_Q_HEADS, qkg_ref[0:1, :], ATT_SCALE * LOG2E)
    heads_to(k_ref, OFF_K, N_KV_HEADS, qkg_ref[1:2, :], 1.0)

    proj_to(z_ref, OFF_Z, D_INNER)
    proj_to(xbc_ref, OFF_XBC, CONV_DIM)
    vv = proj(OFF_V, OFF_GS).astype(BF16)
    for hh in range(N_KV_HEADS):
        lo = 2 * hh * ATT_HEAD_DIM
        v_ref[:, lo:lo + ATT_HEAD_DIM] = vv[:, hh * ATT_HEAD_DIM:(hh + 1) * ATT_HEAD_DIM]
        v_ref[:, lo + ATT_HEAD_DIM:lo + 2 * ATT_HEAD_DIM] = jnp.ones((x_ref.shape[0], ATT_HEAD_DIM), BF16)
    proj_to(gs_ref, OFF_GS, D_MODEL)
    proj_to(ga_ref, OFF_GA, D_MODEL)
    dt_ref[...] = proj(OFF_DT, IN_COLS)


def inproj(x, mods, row_of_b, ng, w_in, qk_g, cos_t, sin_t, tm):
    b, l, d = x.shape
    tok = lambda b, i: (b, i, 0)
    const = lambda b, i: (0, 0)
    widths = (D_INNER, CONV_DIM, ATT_DIM, KV_DIM, 2 * KV_DIM, D_MODEL, D_MODEL, LANES)
    dtypes = (BF16, BF16, BF16, BF16, BF16, BF16, BF16, F32)
    return pl.pallas_call(
        _inproj_kernel,
        grid=(b, l // tm),
        in_specs=[pl.BlockSpec((None, tm, d), tok),
                  _mod_spec(row_of_b, 3), _mod_spec(row_of_b, 4),
                  pl.BlockSpec((1, d), const),
                  pl.BlockSpec((d, IN_COLS), const, pipeline_mode=pl.Buffered(1)),
                  pl.BlockSpec((2, ATT_HEAD_DIM), const),
                  pl.BlockSpec((tm, ATT_HEAD_DIM), lambda b, i: (i, 0)),
                  pl.BlockSpec((tm, ATT_HEAD_DIM), lambda b, i: (i, 0))],
        out_specs=[pl.BlockSpec((None, tm, w), tok) for w in widths],
        out_shape=[jax.ShapeDtypeStruct((b, l, w), dt) for w, dt in zip(widths, dtypes)],
        compiler_params=_cparams(("arbitrary", "arbitrary")),
        name="inproj",
    )(x, mods, mods, ng, w_in, qk_g, cos_t, sin_t)


def _scan_rows(v, suffix):
    n = v.shape[0]
    row = lax.broadcasted_iota(jnp.int32, v.shape, 0)
    k = 1
    while k < n:
        if suffix:
            v = v + jnp.where(row < n - k, pltpu.roll(v, n - k, axis=0), 0.0)
        else:
            v = v + jnp.where(row >= k, pltpu.roll(v, k, axis=0), 0.0)
        k *= 2
    return v


def _split_bf16(v, low, pieces):
    hi = v.astype(BF16).astype(F32)
    r1 = v - hi
    mid = r1.astype(BF16).astype(F32)
    first = jnp.where(low, hi, mid).astype(BF16)
    if pieces == 2:
        return first
    return jnp.concatenate([first, jnp.where(low, r1 - mid, 0.0).astype(BF16)], axis=1)


def _dtprep_kernel(raw_ref, bias_ref, nega_ref, cs_ref, e_ref, w_ref, rt_ref):
    n_chunks = raw_ref.shape[0] // CHUNK
    lane = lax.broadcasted_iota(jnp.int32, (CHUNK, LANES), 1)
    is_bwd = (lane & HEADS_PER_GROUP) != 0
    low = lane < PIECE
    for c in range(n_chunks):
        rows = slice(c * CHUNK, (c + 1) * CHUNK)
        v = raw_ref[rows, :] + bias_ref[...]
        dt = jnp.maximum(v, 0.0) + jnp.log1p(jnp.exp(-jnp.abs(v)))
        da = dt * nega_ref[...]
        pre = _scan_rows(da, suffix=False)
        cs = jnp.where(is_bwd, _scan_rows(da, suffix=True), pre)
        tot = pre[CHUNK - 1:CHUNK, :]
        cs_ref[rows, :] = _split_bf16(cs * LOG2E, low, 3)
        e_ref[rows, :] = _split_bf16(jnp.exp(cs), low, 2)
        w_ref[rows, :] = _split_bf16(dt * jnp.exp(tot - cs), low, 2)
        rt_ref[c] = ((cs - jnp.log(dt)) * LOG2E).T[0:PIECE, :]


def dtprep(raw, bias_row, nega_row):
    b, l, _ = raw.shape
    r = min(l, 1024)
    tok = lambda b, i: (b, i, 0)
    const = lambda b, i: (0, 0)
    return pl.pallas_call(
        _dtprep_kernel,
        grid=(b, l // r),
        in_specs=[pl.BlockSpec((None, r, LANES), tok), pl.BlockSpec((1, LANES), const),
                  pl.BlockSpec((1, LANES), const)],
        out_specs=[pl.BlockSpec((None, r, 2 * LANES), tok), pl.BlockSpec((None, r, LANES), tok),
                   pl.BlockSpec((None, r, LANES), tok),
                   pl.BlockSpec((None, r // CHUNK, PIECE, CHUNK), lambda b, i: (b, i, 0, 0))],
        out_shape=[jax.ShapeDtypeStruct((b, l, 2 * LANES), BF16), jax.ShapeDtypeStruct((b, l, LANES), BF16),
                   jax.ShapeDtypeStruct((b, l, LANES), BF16),
                   jax.ShapeDtypeStruct((b, l // CHUNK, PIECE, CHUNK), F32)],
        compiler_params=_cparams(("arbitrary", "arbitrary")),
        name="dtprep",
    )(raw, bias_row, nega_row)


def _ssd_kernel(with_ctx_out,
                xs_l, bm_l, cm_l, cs_l, e_l, w_l, rt_l, xs_c, bm_c, cm_c, cs_c, e_c, w_c, rt_c,
                cwx, cwb, cwc, cbx, cbb, cbc, dskip, selcs, expf, expb,
                y_l, y_c,
                xc_l, bt_l, cc_l, sb_l, xc_c, bt_c, cc_c, sb_c, st_f, st_b):
    lat = dict(xs=xs_l, bm=bm_l, cm=cm_l, cs=cs_l, e=e_l, w=w_l, rt=rt_l, y=y_l,
               xc=xc_l, bt=bt_l, cc=cc_l, sb=sb_l, out=True)
    ctx = dict(xs=xs_c, bm=bm_c, cm=cm_c, cs=cs_c, e=e_c, w=w_c, rt=rt_c, y=y_c,
               xc=xc_c, bt=bt_c, cc=cc_c, sb=sb_c, out=with_ctx_out)

    def rows(c):
        return pl.ds(pl.multiple_of(c * CHUNK, CHUNK), CHUNK)

    def bwd_pass(s):
        n_chunks = s["xs"].shape[0] // CHUNK
        row = lax.broadcasted_iota(jnp.int32, (CHUNK, 1), 0)

        def conv(src, w_ref, b_ref, c):
            cur = src[rows(c), :].astype(F32)
            p_at = pl.multiple_of(jnp.maximum(c * CHUNK - 16, 0), 16)
            n_at = pl.multiple_of(jnp.minimum((c + 1) * CHUNK, n_chunks * CHUNK - 16), 16)
            prev = jnp.where(c > 0, src[pl.ds(p_at, 16), :].astype(F32)[15:16, :], 0.0)
            nxt = jnp.where(c < n_chunks - 1, src[pl.ds(n_at, 16), :].astype(F32)[0:1, :], 0.0)
            up = jnp.where(row == 0, prev, pltpu.roll(cur, 1, axis=0))
            dn = jnp.where(row == CHUNK - 1, nxt, pltpu.roll(cur, CHUNK - 1, axis=0))
            return _silu(w_ref[0:1, :] * up + w_ref[1:2, :] * cur + w_ref[2:3, :] * dn + b_ref[...])

        def body(i, carry):
            c = n_chunks - 1 - i
            xcv = conv(s["xs"], cwx, cbx, c)
            bt = conv(s["bm"], cwb, cbb, c).T.astype(BF16)
            s["xc"][rows(c), :] = xcv.astype(BF16)
            s["cc"][rows(c), :] = conv(s["cm"], cwc, cbc, c).astype(BF16)
            s["bt"][c] = bt
            s["sb"][c] = st_b[...].astype(BF16)
            w_b = _mm(s["w"][rows(c), :], expb[...])
            dec = _mm(s["e"][pl.ds(pl.multiple_of(c * CHUNK, CHUNK), 16), :], expb[...])[0:1, :]
            st_b[...] = st_b[...] * dec + _mm(bt, (xcv * w_b).astype(BF16))
            return carry

        lax.fori_loop(0, n_chunks, body, 0, unroll=2)

    def fwd_pass(s):
        n_chunks = s["xs"].shape[0] // CHUNK
        ri = lax.broadcasted_iota(jnp.int32, (CHUNK, CHUNK), 0)
        ci = lax.broadcasted_iota(jnp.int32, (CHUNK, CHUNK), 1)
        lower = ri >= ci
        upper = ri <= ci
        first = lax.broadcasted_iota(jnp.int32, (CHUNK, LANES), 1) < SSD_HEAD_DIM

        def body(c, carry):
            xc = s["xc"][rows(c), :]
            bt = s["bt"][c]
            if s["out"]:
                cc = s["cc"][rows(c), :]
                colb = _mm(s["cs"][rows(c), :], selcs[...])
                rt = s["rt"][c]
                cb = _mm(cc, bt)
                y_f = _mm(cc, st_f[...].astype(BF16))
                y_b = _mm(cc, s["sb"][c])
                blocks, dec = [], []
                for j in range(HEADS_PER_GROUP // 2):
                    col = [colb[:, (4 * j + t) * CHUNK:(4 * j + t + 1) * CHUNK] for t in range(4)]
                    ms = []
                    for t, hh in enumerate((2 * j, 2 * j + 1)):
                        hb = HEADS_PER_GROUP + hh
                        d_f = jnp.exp2(jnp.where(lower, col[t] - rt[hh:hh + 1, :], -jnp.inf))
                        d_b = jnp.exp2(jnp.where(upper, col[2 + t] - rt[hb:hb + 1, :], -jnp.inf))
                        ms.append((cb * (d_f + d_b)).astype(BF16))
                    xblk = xc[:, j * LANES:(j + 1) * LANES]
                    zero = jnp.zeros_like(xblk)
                    x_bd = jnp.concatenate([jnp.where(first, xblk, zero), jnp.where(first, zero, xblk)], axis=0)
                    yd = _mm(jnp.concatenate(ms, axis=1), x_bd)
                    e_f = jnp.exp2(jnp.where(first, col[0], col[1]))
                    e_b = jnp.exp2(jnp.where(first, col[2], col[3]))
                    sl = slice(j * LANES, (j + 1) * LANES)
                    blocks.append(yd + y_f[:, sl] * e_f + y_b[:, sl] * e_b + xblk.astype(F32) * dskip[:, sl])
                    dec.append(e_f[CHUNK - 1:CHUNK, :])
                s["y"][rows(c), :] = jnp.concatenate(blocks, axis=1).astype(BF16)
                dec = jnp.concatenate(dec, axis=1)
            else:
                last16 = pl.ds(pl.multiple_of(c * CHUNK + CHUNK - 16, 16), 16)
                dec = _mm(s["e"][last16, :], expf[...])[15:16, :]
            xw = (xc.astype(F32) * _mm(s["w"][rows(c), :], expf[...])).astype(BF16)
            st_f[...] = st_f[...] * dec + _mm(bt, xw)
            return carry

        lax.fori_loop(0, n_chunks, body, 0, unroll=2)

    st_b[...] = jnp.zeros_like(st_b)
    bwd_pass(ctx)
    bwd_pass(lat)
    st_f[...] = jnp.zeros_like(st_f)
    fwd_pass(ctx)
    fwd_pass(lat)
    if not with_ctx_out:
        y_c[...] = jnp.zeros_like(y_c)


def _head_select_tables():
    g = jnp.arange(SSD_GROUPS)[:, None, None]
    k = jnp.arange(2 * LANES)[None, :, None]
    n = jnp.arange(2 * HEADS_PER_GROUP * CHUNK)[None, None, :]
    q = n // CHUNK
    head_dir = (q % 4 // 2) * HEADS_PER_GROUP + 2 * (q // 4) + q % 2
    selcs = ((k % PIECE) == 2 * HEADS_PER_GROUP * g + head_dir) & (k < 3 * PIECE)
    k2 = jnp.arange(LANES)[None, :, None]
    n2 = jnp.arange(GROUP_DIM)[None, None, :]
    expf = (k2 % PIECE) == 2 * HEADS_PER_GROUP * g + n2 // SSD_HEAD_DIM
    expb = (k2 % PIECE) == 2 * HEADS_PER_GROUP * g + HEADS_PER_GROUP + n2 // SSD_HEAD_DIM
    return selcs.astype(BF16), expf.astype(BF16), expb.astype(BF16)


def ssd(xbc_l, tabs_l, xbc_c, tabs_c, conv_w, conv_b, dskip_row, tables, with_ctx_out):
    b, l, _ = xbc_l.shape
    lc = xbc_c.shape[1]
    n_bg = D_INNER // D_STATE

    def seq_specs(n):
        return [pl.BlockSpec((None, n, GROUP_DIM), lambda b, g: (b, 0, g)),
                pl.BlockSpec((None, n, D_STATE), lambda b, g: (b, 0, n_bg + g)),
                pl.BlockSpec((None, n, D_STATE), lambda b, g: (b, 0, n_bg + SSD_GROUPS + g)),
                pl.BlockSpec((None, n, 2 * LANES), lambda b, g: (b, 0, 0)),
                pl.BlockSpec((None, n, LANES), lambda b, g: (b, 0, 0)),
                pl.BlockSpec((None, n, LANES), lambda b, g: (b, 0, 0)),
                pl.BlockSpec((None, n // CHUNK, 2 * HEADS_PER_GROUP, CHUNK), lambda b, g: (b, 0, g, 0))]

    def seq_scratch(n):
        return [pltpu.VMEM((n, GROUP_DIM), BF16), pltpu.VMEM((n // CHUNK, D_STATE, CHUNK), BF16),
                pltpu.VMEM((n, D_STATE), BF16), pltpu.VMEM((n // CHUNK, D_STATE, GROUP_DIM), BF16)]

    par_specs = [pl.BlockSpec((3, GROUP_DIM), lambda b, g: (0, g)),
                 pl.BlockSpec((3, D_STATE), lambda b, g: (0, n_bg + g)),
                 pl.BlockSpec((3, D_STATE), lambda b, g: (0, n_bg + SSD_GROUPS + g)),
                 pl.BlockSpec((1, GROUP_DIM), lambda b, g: (0, g)),
                 pl.BlockSpec((1, D_STATE), lambda b, g: (0, n_bg + g)),
                 pl.BlockSpec((1, D_STATE), lambda b, g: (0, n_bg + SSD_GROUPS + g)),
                 pl.BlockSpec((1, GROUP_DIM), lambda b, g: (0, g)),
                 pl.BlockSpec((None, 2 * LANES, 2 * HEADS_PER_GROUP * CHUNK), lambda b, g: (g, 0, 0)),
                 pl.BlockSpec((None, LANES, GROUP_DIM), lambda b, g: (g, 0, 0)),
                 pl.BlockSpec((None, LANES, GROUP_DIM), lambda b, g: (g, 0, 0))]
    return pl.pallas_call(
        functools.partial(_ssd_kernel, with_ctx_out),
        grid=(b, SSD_GROUPS),
        in_specs=seq_specs(l) + seq_specs(lc) + par_specs,
        out_specs=[pl.BlockSpec((None, l, GROUP_DIM), lambda b, g: (b, 0, g)),
                   pl.BlockSpec((None, lc, GROUP_DIM), lambda b, g: (b, 0, g))],
        out_shape=[jax.ShapeDtypeStruct((b, l, D_INNER), BF16), jax.ShapeDtypeStruct((b, lc, D_INNER), BF16)],
        scratch_shapes=seq_scratch(l) + seq_scratch(lc) + [pltpu.VMEM((D_STATE, GROUP_DIM), F32)] * 2,
        compiler_params=_cparams(("arbitrary", "arbitrary")),
        name="ssd",
    )(xbc_l, xbc_l, xbc_l, *tabs_l, xbc_c, xbc_c, xbc_c, *tabs_c,
      conv_w, conv_w, conv_w, conv_b, conv_b, conv_b, dskip_row, *tables)


def _attn_kernel(q_ref, k_ref, v_ref, o_ref):
    for h in range(N_Q_HEADS):
        kh = h // Q_PER_KV
        sl = slice(h * ATT_HEAD_DIM, (h + 1) * ATT_HEAD_DIM)
        k = k_ref[:, kh * ATT_HEAD_DIM:(kh + 1) * ATT_HEAD_DIM]
        s = lax.dot_general(q_ref[:, sl], k, (((1,), (1,)), ((), ())), preferred_element_type=F32)
        p = jnp.exp2(s - jnp.max(s, axis=-1, keepdims=True)).astype(BF16)
        o = _mm(p, v_ref[:, 2 * kh * ATT_HEAD_DIM:2 * (kh + 1) * ATT_HEAD_DIM])
        o_ref[:, sl] = (o[:, :ATT_HEAD_DIM] / o[:, ATT_HEAD_DIM:ATT_HEAD_DIM + 1]).astype(BF16)


def attention(q, k_all, v_ones, tq):
    b, l, _ = q.shape
    nk = k_all.shape[1]
    return pl.pallas_call(
        _attn_kernel,
        grid=(b, l // tq),
        in_specs=[pl.BlockSpec((None, tq, ATT_DIM), lambda b, i: (b, i, 0)),
                  pl.BlockSpec((None, nk, KV_DIM), lambda b, i: (b, 0, 0)),
                  pl.BlockSpec((None, nk, 2 * KV_DIM), lambda b, i: (b, 0, 0))],
        out_specs=pl.BlockSpec((None, tq, ATT_DIM), lambda b, i: (b, i, 0)),
        out_shape=jax.ShapeDtypeStruct((b, l, ATT_DIM), BF16),
        compiler_params=_cparams(("arbitrary", "arbitrary")),
        name="attention",
    )(q, k_all, v_ones)


def _merge_kernel(y_ref, z_ref, sng_ref, att_ref, gs_ref, ga_ref, x_ref, g2_ref, wso_ref, wao_ref, wo_ref, o_ref):
    m = jax.nn.sigmoid(ga_ref[...].astype(F32)) * _mm(att_ref[...], wao_ref[...])
    y_ssd = None
    for g in range(SSD_GROUPS):
        sl = slice(g * GROUP_DIM, (g + 1) * GROUP_DIM)
        yg = y_ref[:, sl].astype(F32) * _silu(z_ref[:, sl].astype(F32))
        yn = yg * lax.rsqrt(jnp.mean(yg * yg, axis=-1, keepdims=True) + EPS) * sng_ref[:, sl]
        part = _mm(yn.astype(BF16), wso_ref[sl, :])
        y_ssd = part if y_ssd is None else y_ssd + part
    m = m + jax.nn.sigmoid(gs_ref[...].astype(F32)) * y_ssd
    u = _mm(m.astype(BF16), wo_ref[...])
    o_ref[...] = x_ref[...] + g2_ref[...] * u


def merge(y, z, sng, att, gs, ga, x, mods, row_of_b, wso, wao, wo, tm):
    b, l, d = x.shape
    tok = lambda b, i: (b, i, 0)
    const = lambda b, i: (0, 0)
    return pl.pallas_call(
        _merge_kernel,
        grid=(b, l // tm),
        in_specs=[pl.BlockSpec((None, tm, D_INNER), tok), pl.BlockSpec((None, tm, D_INNER), tok),
                  pl.BlockSpec((1, D_INNER), const), pl.BlockSpec((None, tm, ATT_DIM), tok),
                  pl.BlockSpec((None, tm, d), tok), pl.BlockSpec((None, tm, d), tok),
                  pl.BlockSpec((None, tm, d), tok), _mod_spec(row_of_b, 5),
                  pl.BlockSpec((D_INNER, d), const), pl.BlockSpec((ATT_DIM, d), const),
                  pl.BlockSpec((d, d), const)],
        out_specs=pl.BlockSpec((None, tm, d), tok),
        out_shape=jax.ShapeDtypeStruct(x.shape, F32),
        compiler_params=_cparams(("arbitrary", "arbitrary")),
        name="merge",
    )(y, z, sng, att, gs, ga, x, mods, wso, wao, wo)


def _rope_tables(n_tok):
    rows = n_tok // GRID_W
    row = jnp.broadcast_to(jnp.arange(rows)[:, None], (rows, GRID_W)).reshape(-1).astype(F32)
    col = jnp.broadcast_to(jnp.arange(GRID_W)[None, :], (rows, GRID_W)).reshape(-1).astype(F32)
    inv_freq = ROPE_THETA ** (-(jnp.arange(0, ROPE_AXIS_DIM, 2, dtype=F32) / ROPE_AXIS_DIM))
    ar = row[:, None] * inv_freq
    ac = col[:, None] * inv_freq
    cos_t = jnp.concatenate([jnp.cos(ar), jnp.cos(ar), jnp.cos(ac), jnp.cos(ac)], axis=1)
    sin_t = jnp.concatenate([-jnp.sin(ar), jnp.sin(ar), -jnp.sin(ac), jnp.sin(ac)], axis=1)
    return cos_t, sin_t


def _dt_lanes(fwd, bwd):
    lead = fwd.shape[:-1]
    f = fwd.reshape(*lead, SSD_GROUPS, 1, HEADS_PER_GROUP)
    bk = bwd.reshape(*lead, SSD_GROUPS, 1, HEADS_PER_GROUP)
    half = jnp.concatenate([f, bk], axis=-2).reshape(*lead, PIECE)
    return jnp.concatenate([half, half], axis=-1)


def _permute_w_in(w_in):
    o = D_INNER + CONV_DIM
    dt_cols = _dt_lanes(w_in[..., o:o + SSD_HEADS], w_in[..., o + SSD_HEADS:o + 2 * SSD_HEADS])
    return jnp.concatenate([w_in[..., :o], w_in[..., o + 2 * SSD_HEADS:], dt_cols], axis=-1).astype(BF16)


def kernel(x, c, ctx, c_ctx, w_ada, b_ada, norm_g, ffn1_w13, ffn1_w2, w_in, conv_w, conv_b,
           a_log, dt_bias, d_skip, ssd_norm_g, w_ssd_out, qk_norm_g, w_attn_out, w_out,
           ffn2_w13, ffn2_w2):
    batch, seq, d = x.shape
    lc = ctx.shape[1]
    assert batch + 1 <= MOD_ROWS and seq % 512 == 0 and lc % CHUNK == 0

    cond = jnp.zeros((MOD_ROWS, d), F32).at[:batch].set(c).at[batch].set(c_ctx)
    mods_all = ada_all(cond, w_ada, b_ada).reshape(DEPTH, MOD_ROWS, 1, N_MOD * d)

    cos_l, sin_l = _rope_tables(seq)
    cos_c = jnp.ones((lc, ATT_HEAD_DIM), F32)
    sin_c = jnp.zeros((lc, ATT_HEAD_DIM), F32)

    w_in_p = _permute_w_in(w_in)
    f1_w13, f1_w2 = ffn1_w13.astype(BF16), ffn1_w2.astype(BF16)
    f2_w13, f2_w2 = ffn2_w13.astype(BF16), ffn2_w2.astype(BF16)
    wso, wao, wo = w_ssd_out.astype(BF16), w_attn_out.astype(BF16), w_out.astype(BF16)
    dtb_rows = _dt_lanes(dt_bias[:, 0], dt_bias[:, 1]).reshape(DEPTH, 1, LANES)
    nega = -jnp.exp(a_log.astype(F32))
    nega_rows = _dt_lanes(nega[:, 0], nega[:, 1]).reshape(DEPTH, 1, LANES)
    dskip_rows = jnp.repeat(d_skip, SSD_HEAD_DIM, axis=-1).reshape(DEPTH, 1, D_INNER)
    tables = _head_select_tables()

    lat_row = lambda b: b
    ctx_row = lambda b: batch
    tm_l, tm_c = 512, lc
    xc = ctx
    for i in range(DEPTH):
        last = i == DEPTH - 1
        mods = mods_all[i]
        ng = norm_g[i]
        sng = ssd_norm_g[i].reshape(1, D_INNER)
        x = ffn(x, mods, lat_row, 0, ng[0:1], f1_w13[i], f1_w2[i], tm_l)
        xc = ffn(xc, mods, ctx_row, 0, ng[0:1], f1_w13[i], f1_w2[i], tm_c)

        z, xbc, q, k, v, gs, ga, dt = inproj(x, mods, lat_row, ng[1:2], w_in_p[i], qk_norm_g[i], cos_l, sin_l, tm_l)
        zc, xbcc, qc, kc, vc, gsc, gac, dtc = inproj(xc, mods, ctx_row, ng[1:2], w_in_p[i], qk_norm_g[i],
                                                     cos_c, sin_c, tm_c)
        tabs = dtprep(dt, dtb_rows[i], nega_rows[i])
        tabs_c = dtprep(dtc, dtb_rows[i], nega_rows[i])
        y, yc = ssd(xbc, tabs, xbcc, tabs_c, conv_w[i], conv_b[i].reshape(1, CONV_DIM), dskip_rows[i],
                    tables, not last)
        att = attention(q, jnp.concatenate([kc, k], axis=1), jnp.concatenate([vc, v], axis=1), 256)
        x = merge(y, z, sng, att, gs, ga, x, mods, lat_row, wso[i], wao[i], wo[i], tm_l)
        x = ffn(x, mods, lat_row, 6, ng[2:3], f2_w13[i], f2_w2[i], tm_l)
        if not last:
            attc = attention(qc, kc, vc, lc)
            xc = merge(yc, zc, sng, attc, gsc, gac, xc, mods, ctx_row, wso[i], wao[i], wo[i], tm_c)
            xc = ffn(xc, mods, ctx_row, 6, ng[2:3], f2_w13[i], f2_w2[i], tm_c)
    return x
```

```python
import functools
import math

import jax
import jax.numpy as jnp
from jax import lax
from jax.experimental import pallas as pl
from jax.experimental.pallas import tpu as pltpu

F32 = jnp.float32
BF16 = jnp.bfloat16

D_MODEL = 1024
DEPTH = 4
GRID_W = 64
EPS = 1e-6
N_MOD = 9
D_FF = 2816

D_INNER = 2048
SSD_HEAD_DIM = 64
SSD_HEADS = 32
SSD_GROUPS = 4
HEADS_PER_GROUP = SSD_HEADS // SSD_GROUPS
GROUP_DIM = HEADS_PER_GROUP * SSD_HEAD_DIM
D_STATE = 128
BC_DIM = SSD_GROUPS * D_STATE
CONV_DIM = D_INNER + 2 * BC_DIM
CHUNK = 128

ATT_HEAD_DIM = 128
N_Q_HEADS = 8
N_KV_HEADS = 2
Q_PER_KV = N_Q_HEADS // N_KV_HEADS
ATT_DIM = N_Q_HEADS * ATT_HEAD_DIM
KV_DIM = N_KV_HEADS * ATT_HEAD_DIM
ATT_SCALE = ATT_HEAD_DIM ** -0.5
ROPE_THETA = 10000.0
ROPE_AXIS_DIM = ATT_HEAD_DIM // 2

LANES = 128
PIECE = 2 * SSD_HEADS
LOG2E = math.log2(math.e)
OFF_Z = 0
OFF_XBC = OFF_Z + D_INNER
OFF_Q = OFF_XBC + CONV_DIM
OFF_K = OFF_Q + ATT_DIM
OFF_V = OFF_K + KV_DIM
OFF_GS = OFF_V + KV_DIM
OFF_GA = OFF_GS + D_MODEL
OFF_DT = OFF_GA + D_MODEL
IN_COLS = OFF_DT + LANES

MOD_ROWS = 16
VMEM_LIMIT = 56 * 1024 * 1024


def _cparams(sem):
    return pltpu.CompilerParams(dimension_semantics=sem, vmem_limit_bytes=VMEM_LIMIT)


def _silu(v):
    return v * jax.nn.sigmoid(v)


def _norm_mod(x, ng, sc, sh):
    r = lax.rsqrt(jnp.mean(x * x, axis=-1, keepdims=True) + EPS)
    return (x * r * ng) * (1.0 + sc) + sh


def _mm(a, b):
    return jnp.dot(a, b, preferred_element_type=F32)


def _ada_kernel(cond_ref, w_ref, b_ref, o_ref):
    a = _silu(cond_ref[...])
    o_ref[...] = jnp.dot(a, w_ref[...], preferred_element_type=F32,
                         precision=lax.Precision.HIGHEST) + b_ref[...]


def ada_all(cond, w_ada, b_ada, tn=1024):
    depth, d, n = w_ada.shape
    return pl.pallas_call(
        _ada_kernel,
        grid=(depth, n // tn),
        in_specs=[pl.BlockSpec((MOD_ROWS, d), lambda l, j: (0, 0)),
                  pl.BlockSpec((None, d, tn), lambda l, j: (l, 0, j)),
                  pl.BlockSpec((None, 1, tn), lambda l, j: (l, 0, j))],
        out_specs=pl.BlockSpec((None, MOD_ROWS, tn), lambda l, j: (l, 0, j)),
        out_shape=jax.ShapeDtypeStruct((depth, MOD_ROWS, n), F32),
        compiler_params=_cparams(("arbitrary", "arbitrary")),
        name="ada_params",
    )(cond, w_ada, b_ada.reshape(depth, 1, n))


def _mod_spec(row_of_b, k):
    return pl.BlockSpec((None, 1, D_MODEL), lambda b, i, *_: (row_of_b(b), 0, k))


FF_STEP = 2816


def _ffn_kernel(x_ref, sh_ref, sc_ref, gt_ref, ng_ref, w13_ref, w2_ref, o_ref):
    x = x_ref[...]
    h = _norm_mod(x, ng_ref[...], sc_ref[...], sh_ref[...]).astype(BF16)
    acc = None
    for c in range(0, D_FF, FF_STEP):
        a = _mm(h, w13_ref[:, c:c + FF_STEP])
        g = _mm(h, w13_ref[:, D_FF + c:D_FF + c + FF_STEP])
        part = _mm((_silu(a) * g).astype(BF16), w2_ref[c:c + FF_STEP, :])
        acc = part if acc is None else acc + part
    o_ref[...] = x + 0.5 * gt_ref[...] * acc


def ffn(x, mods, row_of_b, k0, ng, w13, w2, tm):
    b, l, d = x.shape
    tok = lambda b, i: (b, i, 0)
    const = lambda b, i: (0, 0)
    return pl.pallas_call(
        _ffn_kernel,
        grid=(b, l // tm),
        in_specs=[pl.BlockSpec((None, tm, d), tok),
                  _mod_spec(row_of_b, k0), _mod_spec(row_of_b, k0 + 1), _mod_spec(row_of_b, k0 + 2),
                  pl.BlockSpec((1, d), const),
                  pl.BlockSpec((d, 2 * D_FF), const, pipeline_mode=pl.Buffered(1)),
                  pl.BlockSpec((D_FF, d), const, pipeline_mode=pl.Buffered(1))],
        out_specs=pl.BlockSpec((None, tm, d), tok),
        out_shape=jax.ShapeDtypeStruct(x.shape, F32),
        compiler_params=_cparams(("arbitrary", "arbitrary")),
        name="ffn",
    )(x, mods, mods, mods, ng, w13, w2)


def _swap32(v):
    lane = lax.broadcasted_iota(jnp.int32, v.shape, 1)
    return jnp.where((lane & 32) == 0, pltpu.roll(v, LANES - 32, axis=1), pltpu.roll(v, 32, axis=1))


def _inproj_kernel(x_ref, sh_ref, sc_ref, ng_ref, w_ref, qkg_ref, cos_ref, sin_ref,
                   z_ref, q_ref, k_ref, v_ref, gs_ref, ga_ref, dt_ref, xs_ref, bm_ref, cm_ref):
    h = _norm_mod(x_ref[...], ng_ref[...], sc_ref[...], sh_ref[...]).astype(BF16)

    def proj(lo, hi):
        return _mm(h, w_ref[:, lo:hi])

    def proj_to(o_ref, lo, width, step=1024):
        for c in range(0, width, step):
            o_ref[:, c:c + step] = proj(lo + c, lo + c + step).astype(BF16)

    cos = cos_ref[...]
    sin = sin_ref[...]

    def heads_to(o_ref, lo, n_heads, gain, scale):
        for h0 in range(0, n_heads, 2):
            pair = proj(lo + h0 * ATT_HEAD_DIM, lo + (h0 + 2) * ATT_HEAD_DIM)
            for hh in (h0, h0 + 1):
                t = pair[:, (hh - h0) * ATT_HEAD_DIM:(hh - h0 + 1) * ATT_HEAD_DIM]
                t = t * lax.rsqrt(jnp.mean(t * t, axis=-1, keepdims=True) + EPS) * gain
                t = t * cos + _swap32(t) * sin
                t = t * scale if scale != 1.0 else t
                o_ref[:, hh * ATT_HEAD_DIM:(hh + 1) * ATT_HEAD_DIM] = t.astype(BF16)

    heads_to(q_ref, OFF_Q, N_Q_HEADS, qkg_ref[0:1, :], ATT_SCALE * LOG2E)
    heads_to(k_ref, OFF_K, N_KV_HEADS, qkg_ref[1:2, :], 1.0)

    proj_to(z_ref, OFF_Z, D_INNER)
    for g in range(SSD_GROUPS):
        xs_ref[g] = proj(OFF_XBC + g * GROUP_DIM, OFF_XBC + (g + 1) * GROUP_DIM).astype(BF16)
    bc = proj(OFF_XBC + D_INNER, OFF_Q).astype(BF16)
    for g in range(SSD_GROUPS):
        bm_ref[g] = bc[:, g * D_STATE:(g + 1) * D_STATE]
        cm_ref[g] = bc[:, BC_DIM + g * D_STATE:BC_DIM + (g + 1) * D_STATE]
    vv = proj(OFF_V, OFF_GS).astype(BF16)
    for hh in range(N_KV_HEADS):
        lo = 2 * hh * ATT_HEAD_DIM
        v_ref[:, lo:lo + ATT_HEAD_DIM] = vv[:, hh * ATT_HEAD_DIM:(hh + 1) * ATT_HEAD_DIM]
        v_ref[:, lo + ATT_HEAD_DIM:lo + 2 * ATT_HEAD_DIM] = jnp.ones((x_ref.shape[0], ATT_HEAD_DIM), BF16)
    proj_to(gs_ref, OFF_GS, D_MODEL)
    proj_to(ga_ref, OFF_GA, D_MODEL)
    dt_ref[...] = proj(OFF_DT, IN_COLS)


def inproj(x, mods, row_of_b, ng, w_in, qk_g, cos_t, sin_t, tm):
    b, l, d = x.shape
    tok = lambda b, i: (b, i, 0)
    const = lambda b, i: (0, 0)
    widths = (D_INNER, ATT_DIM, KV_DIM, 2 * KV_DIM, D_MODEL, D_MODEL, LANES)
    dtypes = (BF16, BF16, BF16, BF16, BF16, BF16, F32)
    g_widths = (GROUP_DIM, D_STATE, D_STATE)
    return pl.pallas_call(
        _inproj_kernel,
        grid=(b, l // tm),
        in_specs=[pl.BlockSpec((None, tm, d), tok),
                  _mod_spec(row_of_b, 3), _mod_spec(row_of_b, 4),
                  pl.BlockSpec((1, d), const),
                  pl.BlockSpec((d, IN_COLS), const, pipeline_mode=pl.Buffered(1)),
                  pl.BlockSpec((2, ATT_HEAD_DIM), const),
                  pl.BlockSpec((tm, ATT_HEAD_DIM), lambda b, i: (i, 0)),
                  pl.BlockSpec((tm, ATT_HEAD_DIM), lambda b, i: (i, 0))],
        out_specs=[pl.BlockSpec((None, tm, w), tok) for w in widths]
        + [pl.BlockSpec((None, SSD_GROUPS, tm, w), lambda b, i: (b, 0, i, 0)) for w in g_widths],
        out_shape=[jax.ShapeDtypeStruct((b, l, w), dt) for w, dt in zip(widths, dtypes)]
        + [jax.ShapeDtypeStruct((b, SSD_GROUPS, l, w), BF16) for w in g_widths],
        compiler_params=_cparams(("arbitrary", "arbitrary")),
        name="inproj",
    )(x, mods, mods, ng, w_in, qk_g, cos_t, sin_t)


def _scan_rows(v, suffix):
    n = v.shape[0]
    row = lax.broadcasted_iota(jnp.int32, v.shape, 0)
    k = 1
    while k < n:
        if suffix:
            v = v + jnp.where(row < n - k, pltpu.roll(v, n - k, axis=0), 0.0)
        else:
            v = v + jnp.where(row >= k, pltpu.roll(v, k, axis=0), 0.0)
        k *= 2
    return v


def _split_bf16(v, low, pieces):
    hi = v.astype(BF16).astype(F32)
    r1 = v - hi
    mid = r1.astype(BF16).astype(F32)
    first = jnp.where(low, hi, mid).astype(BF16)
    if pieces == 2:
        return first
    return jnp.concatenate([first, jnp.where(low, r1 - mid, 0.0).astype(BF16)], axis=1)


def _dtprep_kernel(raw_ref, bias_ref, nega_ref, cs_ref, e_ref, w_ref, rt_ref):
    n_chunks = raw_ref.shape[0] // CHUNK
    lane = lax.broadcasted_iota(jnp.int32, (CHUNK, LANES), 1)
    is_bwd = (lane & HEADS_PER_GROUP) != 0
    low = lane < PIECE
    for c in range(n_chunks):
        rows = slice(c * CHUNK, (c + 1) * CHUNK)
        v = raw_ref[rows, :] + bias_ref[...]
        dt = jnp.maximum(v, 0.0) + jnp.log1p(jnp.exp(-jnp.abs(v)))
        da = dt * nega_ref[...]
        pre = _scan_rows(da, suffix=False)
        cs = jnp.where(is_bwd, _scan_rows(da, suffix=True), pre)
        tot = pre[CHUNK - 1:CHUNK, :]
        cs_ref[rows, :] = _split_bf16(cs * LOG2E, low, 3)
        e_ref[rows, :] = _split_bf16(jnp.exp(cs), low, 2)
        w_ref[rows, :] = _split_bf16(dt * jnp.exp(tot - cs), low, 2)
        rt_ref[c] = ((cs - jnp.log(dt)) * LOG2E).T[0:PIECE, :]


def dtprep(raw, bias_row, nega_row):
    b, l, _ = raw.shape
    r = min(l, 1024)
    tok = lambda b, i: (b, i, 0)
    const = lambda b, i: (0, 0)
    return pl.pallas_call(
        _dtprep_kernel,
        grid=(b, l // r),
        in_specs=[pl.BlockSpec((None, r, LANES), tok), pl.BlockSpec((1, LANES), const),
                  pl.BlockSpec((1, LANES), const)],
        out_specs=[pl.BlockSpec((None, r, 2 * LANES), tok), pl.BlockSpec((None, r, LANES), tok),
                   pl.BlockSpec((None, r, LANES), tok),
                   pl.BlockSpec((None, r // CHUNK, PIECE, CHUNK), lambda b, i: (b, i, 0, 0))],
        out_shape=[jax.ShapeDtypeStruct((b, l, 2 * LANES), BF16), jax.ShapeDtypeStruct((b, l, LANES), BF16),
                   jax.ShapeDtypeStruct((b, l, LANES), BF16),
                   jax.ShapeDtypeStruct((b, l // CHUNK, PIECE, CHUNK), F32)],
        compiler_params=_cparams(("arbitrary", "arbitrary")),
        name="dtprep",
    )(raw, bias_row, nega_row)


def _ssd_kernel(with_ctx_out,
                xs_l, bm_l, cm_l, cs_l, e_l, w_l, rt_l, xs_c, bm_c, cm_c, cs_c, e_c, w_c, rt_c,
                cwx, cwb, cwc, cbx, cbb, cbc, dskip, selcs, expf, expb,
                y_l, y_c,
                xc_l, bt_l, cc_l, sb_l, xc_c, bt_c, cc_c, sb_c, st_f, st_b):
    lat = dict(xs=xs_l, bm=bm_l, cm=cm_l, cs=cs_l, e=e_l, w=w_l, rt=rt_l, y=y_l,
               xc=xc_l, bt=bt_l, cc=cc_l, sb=sb_l, out=True)
    ctx = dict(xs=xs_c, bm=bm_c, cm=cm_c, cs=cs_c, e=e_c, w=w_c, rt=rt_c, y=y_c,
               xc=xc_c, bt=bt_c, cc=cc_c, sb=sb_c, out=with_ctx_out)

    def rows(c):
        return pl.ds(pl.multiple_of(c * CHUNK, CHUNK), CHUNK)

    def bwd_pass(s):
        n_chunks = s["xs"].shape[0] // CHUNK
        row = lax.broadcasted_iota(jnp.int32, (CHUNK, 1), 0)

        def conv(src, w_ref, b_ref, c):
            cur = src[rows(c), :].astype(F32)
            p_at = pl.multiple_of(jnp.maximum(c * CHUNK - 16, 0), 16)
            n_at = pl.multiple_of(jnp.minimum((c + 1) * CHUNK, n_chunks * CHUNK - 16), 16)
            prev = jnp.where(c > 0, src[pl.ds(p_at, 16), :].astype(F32)[15:16, :], 0.0)
            nxt = jnp.where(c < n_chunks - 1, src[pl.ds(n_at, 16), :].astype(F32)[0:1, :], 0.0)
            up = jnp.where(row == 0, prev, pltpu.roll(cur, 1, axis=0))
            dn = jnp.where(row == CHUNK - 1, nxt, pltpu.roll(cur, CHUNK - 1, axis=0))
            return _silu(w_ref[0:1, :] * up + w_ref[1:2, :] * cur + w_ref[2:3, :] * dn + b_ref[...])

        def body(i, carry):
            c = n_chunks - 1 - i
            xcv = conv(s["xs"], cwx, cbx, c)
            bt = conv(s["bm"], cwb, cbb, c).T.astype(BF16)
            s["xc"][rows(c), :] = xcv.astype(BF16)
            s["cc"][rows(c), :] = conv(s["cm"], cwc, cbc, c).astype(BF16)
            s["bt"][c] = bt
            s["sb"][c] = st_b[...].astype(BF16)
            w_b = _mm(s["w"][rows(c), :], expb[...])
            dec = _mm(s["e"][pl.ds(pl.multiple_of(c * CHUNK, CHUNK), 16), :], expb[...])[0:1, :]
            st_b[...] = st_b[...] * dec + _mm(bt, (xcv * w_b).astype(BF16))
            return carry

        lax.fori_loop(0, n_chunks, body, 0, unroll=2)

    def fwd_pass(s):
        n_chunks = s["xs"].shape[0] // CHUNK
        ri = lax.broadcasted_iota(jnp.int32, (CHUNK, CHUNK), 0)
        ci = lax.broadcasted_iota(jnp.int32, (CHUNK, CHUNK), 1)
        lower = ri >= ci
        upper = ri <= ci
        first = lax.broadcasted_iota(jnp.int32, (CHUNK, LANES), 1) < SSD_HEAD_DIM

        def body(c, carry):
            xc = s["xc"][rows(c), :]
            bt = s["bt"][c]
            if s["out"]:
                cc = s["cc"][rows(c), :]
                colb = _mm(s["cs"][rows(c), :], selcs[...])
                rt = s["rt"][c]
                cb = _mm(cc, bt)
                y_f = _mm(cc, st_f[...].astype(BF16))
                y_b = _mm(cc, s["sb"][c])
                blocks, dec = [], []
                for j in range(HEADS_PER_GROUP // 2):
                    col = [colb[:, (4 * j + t) * CHUNK:(4 * j + t + 1) * CHUNK] for t in range(4)]
                    ms = []
                    for t, hh in enumerate((2 * j, 2 * j + 1)):
                        hb = HEADS_PER_GROUP + hh
                        d_f = jnp.exp2(jnp.where(lower, col[t] - rt[hh:hh + 1, :], -jnp.inf))
                        d_b = jnp.exp2(jnp.where(upper, col[2 + t] - rt[hb:hb + 1, :], -jnp.inf))
                        ms.append((cb * (d_f + d_b)).astype(BF16))
                    xblk = xc[:, j * LANES:(j + 1) * LANES]
                    zero = jnp.zeros_like(xblk)
                    x_bd = jnp.concatenate([jnp.where(first, xblk, zero), jnp.where(first, zero, xblk)], axis=0)
                    yd = _mm(jnp.concatenate(ms, axis=1), x_bd)
                    e_f = jnp.exp2(jnp.where(first, col[0], col[1]))
                    e_b = jnp.exp2(jnp.where(first, col[2], col[3]))
                    sl = slice(j * LANES, (j + 1) * LANES)
                    blocks.append(yd + y_f[:, sl] * e_f + y_b[:, sl] * e_b + xblk.astype(F32) * dskip[:, sl])
                    dec.append(e_f[CHUNK - 1:CHUNK, :])
                s["y"][rows(c), :] = jnp.concatenate(blocks, axis=1).astype(BF16)
                dec = jnp.concatenate(dec, axis=1)
            else:
                last16 = pl.ds(pl.multiple_of(c * CHUNK + CHUNK - 16, 16), 16)
                dec = _mm(s["e"][last16, :], expf[...])[15:16, :]
            xw = (xc.astype(F32) * _mm(s["w"][rows(c), :], expf[...])).astype(BF16)
            st_f[...] = st_f[...] * dec + _mm(bt, xw)
            return carry

        lax.fori_loop(0, n_chunks, body, 0, unroll=2)

    st_b[...] = jnp.zeros_like(st_b)
    bwd_pass(ctx)
    bwd_pass(lat)
    st_f[...] = jnp.zeros_like(st_f)
    fwd_pass(ctx)
    fwd_pass(lat)
    if not with_ctx_out:
        y_c[...] = jnp.zeros_like(y_c)


def _head_select_tables():
    g = jnp.arange(SSD_GROUPS)[:, None, None]
    k = jnp.arange(2 * LANES)[None, :, None]
    n = jnp.arange(2 * HEADS_PER_GROUP * CHUNK)[None, None, :]
    q = n // CHUNK
    head_dir = (q % 4 // 2) * HEADS_PER_GROUP + 2 * (q // 4) + q % 2
    selcs = ((k % PIECE) == 2 * HEADS_PER_GROUP * g + head_dir) & (k < 3 * PIECE)
    k2 = jnp.arange(LANES)[None, :, None]
    n2 = jnp.arange(GROUP_DIM)[None, None, :]
    expf = (k2 % PIECE) == 2 * HEADS_PER_GROUP * g + n2 // SSD_HEAD_DIM
    expb = (k2 % PIECE) == 2 * HEADS_PER_GROUP * g + HEADS_PER_GROUP + n2 // SSD_HEAD_DIM
    return selcs.astype(BF16), expf.astype(BF16), expb.astype(BF16)


def ssd(conv_in_l, tabs_l, conv_in_c, tabs_c, conv_w, conv_b, dskip_row, tables, with_ctx_out):
    b, _, l, _ = conv_in_l[0].shape
    lc = conv_in_c[0].shape[2]
    n_bg = D_INNER // D_STATE
    slab = lambda b, g: (b, g, 0, 0)

    def seq_specs(n):
        return [pl.BlockSpec((None, None, n, GROUP_DIM), slab),
                pl.BlockSpec((None, None, n, D_STATE), slab),
                pl.BlockSpec((None, None, n, D_STATE), slab),
                pl.BlockSpec((None, n, 2 * LANES), lambda b, g: (b, 0, 0)),
                pl.BlockSpec((None, n, LANES), lambda b, g: (b, 0, 0)),
                pl.BlockSpec((None, n, LANES), lambda b, g: (b, 0, 0)),
                pl.BlockSpec((None, n // CHUNK, 2 * HEADS_PER_GROUP, CHUNK), lambda b, g: (b, 0, g, 0))]

    def seq_scratch(n):
        return [pltpu.VMEM((n, GROUP_DIM), BF16), pltpu.VMEM((n // CHUNK, D_STATE, CHUNK), BF16),
                pltpu.VMEM((n, D_STATE), BF16), pltpu.VMEM((n // CHUNK, D_STATE, GROUP_DIM), BF16)]

    par_specs = [pl.BlockSpec((3, GROUP_DIM), lambda b, g: (0, g)),
                 pl.BlockSpec((3, D_STATE), lambda b, g: (0, n_bg + g)),
                 pl.BlockSpec((3, D_STATE), lambda b, g: (0, n_bg + SSD_GROUPS + g)),
                 pl.BlockSpec((1, GROUP_DIM), lambda b, g: (0, g)),
                 pl.BlockSpec((1, D_STATE), lambda b, g: (0, n_bg + g)),
                 pl.BlockSpec((1, D_STATE), lambda b, g: (0, n_bg + SSD_GROUPS + g)),
                 pl.BlockSpec((1, GROUP_DIM), lambda b, g: (0, g)),
                 pl.BlockSpec((None, 2 * LANES, 2 * HEADS_PER_GROUP * CHUNK), lambda b, g: (g, 0, 0)),
                 pl.BlockSpec((None, LANES, GROUP_DIM), lambda b, g: (g, 0, 0)),
                 pl.BlockSpec((None, LANES, GROUP_DIM), lambda b, g: (g, 0, 0))]
    return pl.pallas_call(
        functools.partial(_ssd_kernel, with_ctx_out),
        grid=(b, SSD_GROUPS),
        in_specs=seq_specs(l) + seq_specs(lc) + par_specs,
        out_specs=[pl.BlockSpec((None, None, l, GROUP_DIM), slab),
                   pl.BlockSpec((None, None, lc, GROUP_DIM), slab)],
        out_shape=[jax.ShapeDtypeStruct((b, SSD_GROUPS, l, GROUP_DIM), BF16),
                   jax.ShapeDtypeStruct((b, SSD_GROUPS, lc, GROUP_DIM), BF16)],
        scratch_shapes=seq_scratch(l) + seq_scratch(lc) + [pltpu.VMEM((D_STATE, GROUP_DIM), F32)] * 2,
        compiler_params=_cparams(("arbitrary", "arbitrary")),
        name="ssd",
    )(*conv_in_l, *tabs_l, *conv_in_c, *tabs_c,
      conv_w, conv_w, conv_w, conv_b, conv_b, conv_b, dskip_row, *tables)


def _attn_kernel(q_ref, k_ref, v_ref, o_ref):
    def scores(h):
        kh = h // Q_PER_KV
        return lax.dot_general(q_ref[:, h * ATT_HEAD_DIM:(h + 1) * ATT_HEAD_DIM],
                               k_ref[:, kh * ATT_HEAD_DIM:(kh + 1) * ATT_HEAD_DIM],
                               (((1,), (1,)), ((), ())), preferred_element_type=F32)

    def finish(h, s):
        kh = h // Q_PER_KV
        p = jnp.exp2(s - jnp.max(s, axis=-1, keepdims=True)).astype(BF16)
        o = _mm(p, v_ref[:, 2 * kh * ATT_HEAD_DIM:2 * (kh + 1) * ATT_HEAD_DIM])
        o_ref[:, h * ATT_HEAD_DIM:(h + 1) * ATT_HEAD_DIM] = (
            o[:, :ATT_HEAD_DIM] / o[:, ATT_HEAD_DIM:ATT_HEAD_DIM + 1]).astype(BF16)

    group = 4
    for h0 in range(0, N_Q_HEADS, group):
        ss = [scores(h) for h in range(h0, h0 + group)]
        for h, s in zip(range(h0, h0 + group), ss):
            finish(h, s)


def attention(q, k_all, v_ones, tq):
    b, l, _ = q.shape
    nk = k_all.shape[1]
    return pl.pallas_call(
        _attn_kernel,
        grid=(b, l // tq),
        in_specs=[pl.BlockSpec((None, tq, ATT_DIM), lambda b, i: (b, i, 0)),
                  pl.BlockSpec((None, nk, KV_DIM), lambda b, i: (b, 0, 0)),
                  pl.BlockSpec((None, nk, 2 * KV_DIM), lambda b, i: (b, 0, 0))],
        out_specs=pl.BlockSpec((None, tq, ATT_DIM), lambda b, i: (b, i, 0)),
        out_shape=jax.ShapeDtypeStruct((b, l, ATT_DIM), BF16),
        compiler_params=_cparams(("arbitrary", "arbitrary")),
        name="attention",
    )(q, k_all, v_ones)


def _merge_kernel(y_ref, z_ref, sng_ref, att_ref, gs_ref, ga_ref, x_ref, g2_ref, wso_ref, wao_ref, wo_ref, o_ref):
    m = jax.nn.sigmoid(ga_ref[...].astype(F32)) * _mm(att_ref[...], wao_ref[...])
    y_ssd = None
    for g in range(SSD_GROUPS):
        sl = slice(g * GROUP_DIM, (g + 1) * GROUP_DIM)
        yg = y_ref[g].astype(F32) * _silu(z_ref[:, sl].astype(F32))
        yn = yg * lax.rsqrt(jnp.mean(yg * yg, axis=-1, keepdims=True) + EPS) * sng_ref[:, sl]
        part = _mm(yn.astype(BF16), wso_ref[sl, :])
        y_ssd = part if y_ssd is None else y_ssd + part
    m = m + jax.nn.sigmoid(gs_ref[...].astype(F32)) * y_ssd
    u = _mm(m.astype(BF16), wo_ref[...])
    o_ref[...] = x_ref[...] + g2_ref[...] * u


def merge(y, z, sng, att, gs, ga, x, mods, row_of_b, wso, wao, wo, tm):
    b, l, d = x.shape
    tok = lambda b, i: (b, i, 0)
    const = lambda b, i: (0, 0)
    return pl.pallas_call(
        _merge_kernel,
        grid=(b, l // tm),
        in_specs=[pl.BlockSpec((None, SSD_GROUPS, tm, GROUP_DIM), lambda b, i: (b, 0, i, 0)),
                  pl.BlockSpec((None, tm, D_INNER), tok),
                  pl.BlockSpec((1, D_INNER), const), pl.BlockSpec((None, tm, ATT_DIM), tok),
                  pl.BlockSpec((None, tm, d), tok), pl.BlockSpec((None, tm, d), tok),
                  pl.BlockSpec((None, tm, d), tok), _mod_spec(row_of_b, 5),
                  pl.BlockSpec((D_INNER, d), const), pl.BlockSpec((ATT_DIM, d), const),
                  pl.BlockSpec((d, d), const)],
        out_specs=pl.BlockSpec((None, tm, d), tok),
        out_shape=jax.ShapeDtypeStruct(x.shape, F32),
        compiler_params=_cparams(("arbitrary", "arbitrary")),
        name="merge",
    )(y, z, sng, att, gs, ga, x, mods, wso, wao, wo)


def _rope_tables(n_tok):
    rows = n_tok // GRID_W
    row = jnp.broadcast_to(jnp.arange(rows)[:, None], (rows, GRID_W)).reshape(-1).astype(F32)
    col = jnp.broadcast_to(jnp.arange(GRID_W)[None, :], (rows, GRID_W)).reshape(-1).astype(F32)
    inv_freq = ROPE_THETA ** (-(jnp.arange(0, ROPE_AXIS_DIM, 2, dtype=F32) / ROPE_AXIS_DIM))
    ar = row[:, None] * inv_freq
    ac = col[:, None] * inv_freq
    cos_t = jnp.concatenate([jnp.cos(ar), jnp.cos(ar), jnp.cos(ac), jnp.cos(ac)], axis=1)
    sin_t = jnp.concatenate([-jnp.sin(ar), jnp.sin(ar), -jnp.sin(ac), jnp.sin(ac)], axis=1)
    return cos_t, sin_t


def _dt_lanes(fwd, bwd):
    lead = fwd.shape[:-1]
    f = fwd.reshape(*lead, SSD_GROUPS, 1, HEADS_PER_GROUP)
    bk = bwd.reshape(*lead, SSD_GROUPS, 1, HEADS_PER_GROUP)
    half = jnp.concatenate([f, bk], axis=-2).reshape(*lead, PIECE)
    return jnp.concatenate([half, half], axis=-1)


def _permute_w_in(w_in):
    o = D_INNER + CONV_DIM
    dt_cols = _dt_lanes(w_in[..., o:o + SSD_HEADS], w_in[..., o + SSD_HEADS:o + 2 * SSD_HEADS])
    return jnp.concatenate([w_in[..., :o], w_in[..., o + 2 * SSD_HEADS:], dt_cols], axis=-1).astype(BF16)


def kernel(x, c, ctx, c_ctx, w_ada, b_ada, norm_g, ffn1_w13, ffn1_w2, w_in, conv_w, conv_b,
           a_log, dt_bias, d_skip, ssd_norm_g, w_ssd_out, qk_norm_g, w_attn_out, w_out,
           ffn2_w13, ffn2_w2):
    batch, seq, d = x.shape
    lc = ctx.shape[1]
    assert batch + 1 <= MOD_ROWS and seq % 512 == 0 and lc % CHUNK == 0

    cond = jnp.zeros((MOD_ROWS, d), F32).at[:batch].set(c).at[batch].set(c_ctx)
    mods_all = ada_all(cond, w_ada, b_ada).reshape(DEPTH, MOD_ROWS, 1, N_MOD * d)

    cos_l, sin_l = _rope_tables(seq)
    cos_c = jnp.ones((lc, ATT_HEAD_DIM), F32)
    sin_c = jnp.zeros((lc, ATT_HEAD_DIM), F32)

    w_in_p = _permute_w_in(w_in)
    f1_w13, f1_w2 = ffn1_w13.astype(BF16), ffn1_w2.astype(BF16)
    f2_w13, f2_w2 = ffn2_w13.astype(BF16), ffn2_w2.astype(BF16)
    wso, wao, wo = w_ssd_out.astype(BF16), w_attn_out.astype(BF16), w_out.astype(BF16)
    dtb_rows = _dt_lanes(dt_bias[:, 0], dt_bias[:, 1]).reshape(DEPTH, 1, LANES)
    nega = -jnp.exp(a_log.astype(F32))
    nega_rows = _dt_lanes(nega[:, 0], nega[:, 1]).reshape(DEPTH, 1, LANES)
    dskip_rows = jnp.repeat(d_skip, SSD_HEAD_DIM, axis=-1).reshape(DEPTH, 1, D_INNER)
    tables = _head_select_tables()

    lat_row = lambda b: b
    ctx_row = lambda b: batch
    tm_l, tm_c = 512, lc
    xc = ctx
    for i in range(DEPTH):
        last = i == DEPTH - 1
        mods = mods_all[i]
        ng = norm_g[i]
        sng = ssd_norm_g[i].reshape(1, D_INNER)
        x = ffn(x, mods, lat_row, 0, ng[0:1], f1_w13[i], f1_w2[i], tm_l)
        xc = ffn(xc, mods, ctx_row, 0, ng[0:1], f1_w13[i], f1_w2[i], tm_c)

        z, q, k, v, gs, ga, dt, *conv_in = inproj(x, mods, lat_row, ng[1:2], w_in_p[i], qk_norm_g[i], cos_l, sin_l, tm_l)
        zc, qc, kc, vc, gsc, gac, dtc, *conv_in_c = inproj(xc, mods, ctx_row, ng[1:2], w_in_p[i], qk_norm_g[i],
                                                           cos_c, sin_c, tm_c)
        tabs = dtprep(dt, dtb_rows[i], nega_rows[i])
        tabs_c = dtprep(dtc, dtb_rows[i], nega_rows[i])
        y, yc = ssd(conv_in, tabs, conv_in_c, tabs_c, conv_w[i], conv_b[i].reshape(1, CONV_DIM), dskip_rows[i],
                    tables, not last)
        att = attention(q, jnp.concatenate([kc, k], axis=1), jnp.concatenate([vc, v], axis=1), 256)
        x = merge(y, z, sng, att, gs, ga, x, mods, lat_row, wso[i], wao[i], wo[i], tm_l)
        x = ffn(x, mods, lat_row, 6, ng[2:3], f2_w13[i], f2_w2[i], tm_l)
        if not last:
            attc = attention(qc, kc, vc, lc)
            xc = merge(yc, zc, sng, attc, gsc, gac, xc, mods, ctx_row, wso[i], wao[i], wo[i], tm_c)
            xc = ffn(xc, mods, ctx_row, 6, ng[2:3], f2_w13[i], f2_w2[i], tm_c)
    return x
```

```python
import functools
import math

import jax
import jax.numpy as jnp
from jax import lax
from jax.experimental import pallas as pl
from jax.experimental.pallas import tpu as pltpu

F32 = jnp.float32
BF16 = jnp.bfloat16

D_MODEL = 1024
DEPTH = 4
GRID_W = 64
EPS = 1e-6
N_MOD = 9
D_FF = 2816

D_INNER = 2048
SSD_HEAD_DIM = 64
SSD_HEADS = 32
SSD_GROUPS = 4
HEADS_PER_GROUP = SSD_HEADS // SSD_GROUPS
GROUP_DIM = HEADS_PER_GROUP * SSD_HEAD_DIM
D_STATE = 128
BC_DIM = SSD_GROUPS * D_STATE
CONV_DIM = D_INNER + 2 * BC_DIM
CHUNK = 128

ATT_HEAD_DIM = 128
N_Q_HEADS = 8
N_KV_HEADS = 2
Q_PER_KV = N_Q_HEADS // N_KV_HEADS
ATT_DIM = N_Q_HEADS * ATT_HEAD_DIM
KV_DIM = N_KV_HEADS * ATT_HEAD_DIM
ATT_SCALE = ATT_HEAD_DIM ** -0.5
ROPE_THETA = 10000.0
ROPE_AXIS_DIM = ATT_HEAD_DIM // 2

LANES = 128
PIECE = 2 * SSD_HEADS
LOG2E = math.log2(math.e)
OFF_Z = 0
OFF_XBC = OFF_Z + D_INNER
OFF_Q = OFF_XBC + CONV_DIM
OFF_K = OFF_Q + ATT_DIM
OFF_V = OFF_K + KV_DIM
OFF_GS = OFF_V + KV_DIM
OFF_GA = OFF_GS + D_MODEL
OFF_DT = OFF_GA + D_MODEL
IN_COLS = OFF_DT + LANES

MOD_ROWS = 16
VMEM_LIMIT = 56 * 1024 * 1024


def _cparams(sem):
    return pltpu.CompilerParams(dimension_semantics=sem, vmem_limit_bytes=VMEM_LIMIT)


def _sigmoid(v):
    return 0.5 + 0.5 * jnp.tanh(0.5 * v)


def _silu(v):
    h = 0.5 * v
    return h + h * jnp.tanh(h)


def _norm_mod(x, ng, sc, sh):
    r = lax.rsqrt(jnp.mean(x * x, axis=-1, keepdims=True) + EPS)
    return (x * r * ng) * (1.0 + sc) + sh


def _mm(a, b):
    return jnp.dot(a, b, preferred_element_type=F32)


def _ada_kernel(cond_ref, w_ref, b_ref, o_ref):
    a = _silu(cond_ref[...])
    o_ref[...] = jnp.dot(a, w_ref[...], preferred_element_type=F32,
                         precision=lax.Precision.HIGHEST) + b_ref[...]


def ada_all(cond, w_ada, b_ada, tn=1024):
    depth, d, n = w_ada.shape
    return pl.pallas_call(
        _ada_kernel,
        grid=(depth, n // tn),
        in_specs=[pl.BlockSpec((MOD_ROWS, d), lambda l, j: (0, 0)),
                  pl.BlockSpec((None, d, tn), lambda l, j: (l, 0, j)),
                  pl.BlockSpec((None, 1, tn), lambda l, j: (l, 0, j))],
        out_specs=pl.BlockSpec((None, MOD_ROWS, tn), lambda l, j: (l, 0, j)),
        out_shape=jax.ShapeDtypeStruct((depth, MOD_ROWS, n), F32),
        compiler_params=_cparams(("arbitrary", "arbitrary")),
        name="ada_params",
    )(cond, w_ada, b_ada.reshape(depth, 1, n))


def _mod_spec(row_of_b, k):
    return pl.BlockSpec((None, 1, D_MODEL), lambda b, i, *_: (row_of_b(b), 0, k))


FF_STEP = 2816


def _ffn_kernel(x_ref, sh_ref, sc_ref, gt_ref, ng_ref, w13_ref, w2_ref, o_ref):
    x = x_ref[...]
    h = _norm_mod(x, ng_ref[...], sc_ref[...], sh_ref[...]).astype(BF16)
    acc = None
    for c in range(0, D_FF, FF_STEP):
        a = _mm(h, w13_ref[:, c:c + FF_STEP])
        g = _mm(h, w13_ref[:, D_FF + c:D_FF + c + FF_STEP])
        part = _mm((_silu(a) * g).astype(BF16), w2_ref[c:c + FF_STEP, :])
        acc = part if acc is None else acc + part
    o_ref[...] = x + 0.5 * gt_ref[...] * acc


def ffn(x, mods, row_of_b, k0, ng, w13, w2, tm):
    b, l, d = x.shape
    tok = lambda b, i: (b, i, 0)
    const = lambda b, i: (0, 0)
    return pl.pallas_call(
        _ffn_kernel,
        grid=(b, l // tm),
        in_specs=[pl.BlockSpec((None, tm, d), tok),
                  _mod_spec(row_of_b, k0), _mod_spec(row_of_b, k0 + 1), _mod_spec(row_of_b, k0 + 2),
                  pl.BlockSpec((1, d), const),
                  pl.BlockSpec((d, 2 * D_FF), const, pipeline_mode=pl.Buffered(1)),
                  pl.BlockSpec((D_FF, d), const, pipeline_mode=pl.Buffered(1))],
        out_specs=pl.BlockSpec((None, tm, d), tok),
        out_shape=jax.ShapeDtypeStruct(x.shape, F32),
        compiler_params=_cparams(("arbitrary", "arbitrary")),
        name="ffn",
    )(x, mods, mods, mods, ng, w13, w2)


def _swap32(v):
    lane = lax.broadcasted_iota(jnp.int32, v.shape, 1)
    return jnp.where((lane & 32) == 0, pltpu.roll(v, LANES - 32, axis=1), pltpu.roll(v, 32, axis=1))


def _inproj_kernel(x_ref, sh_ref, sc_ref, ng_ref, w_ref, qkg_ref, cos_ref, sin_ref,
                   z_ref, q_ref, k_ref, v_ref, gs_ref, ga_ref, dt_ref, xs_ref, bm_ref, cm_ref):
    h = _norm_mod(x_ref[...], ng_ref[...], sc_ref[...], sh_ref[...]).astype(BF16)

    def proj(lo, hi):
        return _mm(h, w_ref[:, lo:hi])

    def proj_to(o_ref, lo, width, step=1024):
        for c in range(0, width, step):
            o_ref[:, c:c + step] = proj(lo + c, lo + c + step).astype(BF16)

    cos = cos_ref[...]
    sin = sin_ref[...]

    def heads_to(o_ref, lo, n_heads, gain, scale):
        for h0 in range(0, n_heads, 2):
            pair = proj(lo + h0 * ATT_HEAD_DIM, lo + (h0 + 2) * ATT_HEAD_DIM)
            for hh in (h0, h0 + 1):
                t = pair[:, (hh - h0) * ATT_HEAD_DIM:(hh - h0 + 1) * ATT_HEAD_DIM]
                t = t * lax.rsqrt(jnp.mean(t * t, axis=-1, keepdims=True) + EPS) * gain
                t = t * cos + _swap32(t) * sin
                t = t * scale if scale != 1.0 else t
                o_ref[:, hh * ATT_HEAD_DIM:(hh + 1) * ATT_HEAD_DIM] = t.astype(BF16)

    heads_to(q_ref, OFF_Q, N_Q_HEADS, qkg_ref[0:1, :], ATT_SCALE * LOG2E)
    heads_to(k_ref, OFF_K, N_KV_HEADS, qkg_ref[1:2, :], 1.0)

    proj_to(z_ref, OFF_Z, D_INNER)
    for g in range(SSD_GROUPS):
        xs_ref[g] = proj(OFF_XBC + g * GROUP_DIM, OFF_XBC + (g + 1) * GROUP_DIM).astype(BF16)
    bc = proj(OFF_XBC + D_INNER, OFF_Q).astype(BF16)
    for g in range(SSD_GROUPS):
        bm_ref[g] = bc[:, g * D_STATE:(g + 1) * D_STATE]
        cm_ref[g] = bc[:, BC_DIM + g * D_STATE:BC_DIM + (g + 1) * D_STATE]
    vv = proj(OFF_V, OFF_GS).astype(BF16)
    for hh in range(N_KV_HEADS):
        lo = 2 * hh * ATT_HEAD_DIM
        v_ref[:, lo:lo + ATT_HEAD_DIM] = vv[:, hh * ATT_HEAD_DIM:(hh + 1) * ATT_HEAD_DIM]
        v_ref[:, lo + ATT_HEAD_DIM:lo + 2 * ATT_HEAD_DIM] = jnp.ones((x_ref.shape[0], ATT_HEAD_DIM), BF16)
    proj_to(gs_ref, OFF_GS, D_MODEL)
    proj_to(ga_ref, OFF_GA, D_MODEL)
    dt_ref[...] = proj(OFF_DT, IN_COLS)


def inproj(x, mods, row_of_b, ng, w_in, qk_g, cos_t, sin_t, tm):
    b, l, d = x.shape
    tok = lambda b, i: (b, i, 0)
    const = lambda b, i: (0, 0)
    widths = (D_INNER, ATT_DIM, KV_DIM, 2 * KV_DIM, D_MODEL, D_MODEL, LANES)
    dtypes = (BF16, BF16, BF16, BF16, BF16, BF16, F32)
    g_widths = (GROUP_DIM, D_STATE, D_STATE)
    return pl.pallas_call(
        _inproj_kernel,
        grid=(b, l // tm),
        in_specs=[pl.BlockSpec((None, tm, d), tok),
                  _mod_spec(row_of_b, 3), _mod_spec(row_of_b, 4),
                  pl.BlockSpec((1, d), const),
                  pl.BlockSpec((d, IN_COLS), const, pipeline_mode=pl.Buffered(1)),
                  pl.BlockSpec((2, ATT_HEAD_DIM), const),
                  pl.BlockSpec((tm, ATT_HEAD_DIM), lambda b, i: (i, 0)),
                  pl.BlockSpec((tm, ATT_HEAD_DIM), lambda b, i: (i, 0))],
        out_specs=[pl.BlockSpec((None, tm, w), tok) for w in widths]
        + [pl.BlockSpec((None, SSD_GROUPS, tm, w), lambda b, i: (b, 0, i, 0)) for w in g_widths],
        out_shape=[jax.ShapeDtypeStruct((b, l, w), dt) for w, dt in zip(widths, dtypes)]
        + [jax.ShapeDtypeStruct((b, SSD_GROUPS, l, w), BF16) for w in g_widths],
        compiler_params=_cparams(("arbitrary", "arbitrary")),
        name="inproj",
    )(x, mods, mods, ng, w_in, qk_g, cos_t, sin_t)


def _scan_rows(v, suffix):
    n = v.shape[0]
    row = lax.broadcasted_iota(jnp.int32, v.shape, 0)
    k = 1
    while k < n:
        if suffix:
            v = v + jnp.where(row < n - k, pltpu.roll(v, n - k, axis=0), 0.0)
        else:
            v = v + jnp.where(row >= k, pltpu.roll(v, k, axis=0), 0.0)
        k *= 2
    return v


def _split_bf16(v, low, pieces):
    hi = v.astype(BF16).astype(F32)
    r1 = v - hi
    mid = r1.astype(BF16).astype(F32)
    first = jnp.where(low, hi, mid).astype(BF16)
    if pieces == 2:
        return first
    return jnp.concatenate([first, jnp.where(low, r1 - mid, 0.0).astype(BF16)], axis=1)


def _dtprep_kernel(raw_ref, bias_ref, nega_ref, cs_ref, e_ref, w_ref, rt_ref):
    n_chunks = raw_ref.shape[0] // CHUNK
    lane = lax.broadcasted_iota(jnp.int32, (CHUNK, LANES), 1)
    is_bwd = (lane & HEADS_PER_GROUP) != 0
    low = lane < PIECE
    for c in range(n_chunks):
        rows = slice(c * CHUNK, (c + 1) * CHUNK)
        v = raw_ref[rows, :] + bias_ref[...]
        dt = jnp.maximum(v, 0.0) + jnp.log1p(jnp.exp(-jnp.abs(v)))
        da = dt * nega_ref[...]
        pre = _scan_rows(da, suffix=False)
        cs = jnp.where(is_bwd, _scan_rows(da, suffix=True), pre)
        tot = pre[CHUNK - 1:CHUNK, :]
        cs_ref[rows, :] = _split_bf16(cs * LOG2E, low, 3)
        e_ref[rows, :] = _split_bf16(jnp.exp(cs), low, 2)
        w_ref[rows, :] = _split_bf16(dt * jnp.exp(tot - cs), low, 2)
        rt_ref[c] = ((cs - jnp.log(dt)) * LOG2E).T[0:PIECE, :]


def dtprep(raw, bias_row, nega_row):
    b, l, _ = raw.shape
    r = min(l, 1024)
    tok = lambda b, i: (b, i, 0)
    const = lambda b, i: (0, 0)
    return pl.pallas_call(
        _dtprep_kernel,
        grid=(b, l // r),
        in_specs=[pl.BlockSpec((None, r, LANES), tok), pl.BlockSpec((1, LANES), const),
                  pl.BlockSpec((1, LANES), const)],
        out_specs=[pl.BlockSpec((None, r, 2 * LANES), tok), pl.BlockSpec((None, r, LANES), tok),
                   pl.BlockSpec((None, r, LANES), tok),
                   pl.BlockSpec((None, r // CHUNK, PIECE, CHUNK), lambda b, i: (b, i, 0, 0))],
        out_shape=[jax.ShapeDtypeStruct((b, l, 2 * LANES), BF16), jax.ShapeDtypeStruct((b, l, LANES), BF16),
                   jax.ShapeDtypeStruct((b, l, LANES), BF16),
                   jax.ShapeDtypeStruct((b, l // CHUNK, PIECE, CHUNK), F32)],
        compiler_params=_cparams(("arbitrary", "arbitrary")),
        name="dtprep",
    )(raw, bias_row, nega_row)


def _ssd_kernel(with_ctx_out,
                xs_l, bm_l, cm_l, cs_l, e_l, w_l, rt_l, xs_c, bm_c, cm_c, cs_c, e_c, w_c, rt_c,
                cwx, cwb, cwc, cbx, cbb, cbc, dskip, selcs, expf, expb,
                y_l, y_c,
                xc_l, bt_l, cc_l, sb_l, xc_c, bt_c, cc_c, sb_c, st_f, st_b):
    lat = dict(xs=xs_l, bm=bm_l, cm=cm_l, cs=cs_l, e=e_l, w=w_l, rt=rt_l, y=y_l,
               xc=xc_l, bt=bt_l, cc=cc_l, sb=sb_l, out=True)
    ctx = dict(xs=xs_c, bm=bm_c, cm=cm_c, cs=cs_c, e=e_c, w=w_c, rt=rt_c, y=y_c,
               xc=xc_c, bt=bt_c, cc=cc_c, sb=sb_c, out=with_ctx_out)

    def rows(c):
        return pl.ds(pl.multiple_of(c * CHUNK, CHUNK), CHUNK)

    def bwd_pass(s):
        n_chunks = s["xs"].shape[0] // CHUNK
        row = lax.broadcasted_iota(jnp.int32, (CHUNK, 1), 0)

        def conv(src, w_ref, b_ref, c):
            cur = src[rows(c), :].astype(F32)
            p_at = pl.multiple_of(jnp.maximum(c * CHUNK - 16, 0), 16)
            n_at = pl.multiple_of(jnp.minimum((c + 1) * CHUNK, n_chunks * CHUNK - 16), 16)
            prev = jnp.where(c > 0, src[pl.ds(p_at, 16), :].astype(F32)[15:16, :], 0.0)
            nxt = jnp.where(c < n_chunks - 1, src[pl.ds(n_at, 16), :].astype(F32)[0:1, :], 0.0)
            up = jnp.where(row == 0, prev, pltpu.roll(cur, 1, axis=0))
            dn = jnp.where(row == CHUNK - 1, nxt, pltpu.roll(cur, CHUNK - 1, axis=0))
            return _silu(w_ref[0:1, :] * up + w_ref[1:2, :] * cur + w_ref[2:3, :] * dn + b_ref[...])

        def body(i, carry):
            c = n_chunks - 1 - i
            xcv = conv(s["xs"], cwx, cbx, c)
            bt = conv(s["bm"], cwb, cbb, c).T.astype(BF16)
            s["xc"][rows(c), :] = xcv.astype(BF16)
            s["cc"][rows(c), :] = conv(s["cm"], cwc, cbc, c).astype(BF16)
            s["bt"][c] = bt
            s["sb"][c] = st_b[...].astype(BF16)
            w_b = _mm(s["w"][rows(c), :], expb[...])
            dec = _mm(s["e"][pl.ds(pl.multiple_of(c * CHUNK, CHUNK), 16), :], expb[...])[0:1, :]
            st_b[...] = st_b[...] * dec + _mm(bt, (xcv * w_b).astype(BF16))
            return carry

        lax.fori_loop(0, n_chunks, body, 0, unroll=2)

    def fwd_pass(s):
        n_chunks = s["xs"].shape[0] // CHUNK
        ri = lax.broadcasted_iota(jnp.int32, (CHUNK, CHUNK), 0)
        ci = lax.broadcasted_iota(jnp.int32, (CHUNK, CHUNK), 1)
        lower = ri >= ci
        upper = ri <= ci
        first = lax.broadcasted_iota(jnp.int32, (CHUNK, LANES), 1) < SSD_HEAD_DIM

        def body(c, carry):
            xc = s["xc"][rows(c), :]
            bt = s["bt"][c]
            if s["out"]:
                cc = s["cc"][rows(c), :]
                colb = _mm(s["cs"][rows(c), :], selcs[...])
                rt = s["rt"][c]
                cb = _mm(cc, bt)
                y_f = _mm(cc, st_f[...].astype(BF16))
                y_b = _mm(cc, s["sb"][c])
                blocks, dec = [], []
                for j in range(HEADS_PER_GROUP // 2):
                    col = [colb[:, (4 * j + t) * CHUNK:(4 * j + t + 1) * CHUNK] for t in range(4)]
                    ms = []
                    for t, hh in enumerate((2 * j, 2 * j + 1)):
                        hb = HEADS_PER_GROUP + hh
                        d_f = jnp.exp2(jnp.where(lower, col[t] - rt[hh:hh + 1, :], -jnp.inf))
                        d_b = jnp.exp2(jnp.where(upper, col[2 + t] - rt[hb:hb + 1, :], -jnp.inf))
                        ms.append((cb * (d_f + d_b)).astype(BF16))
                    xblk = xc[:, j * LANES:(j + 1) * LANES]
                    zero = jnp.zeros_like(xblk)
                    x_bd = jnp.concatenate([jnp.where(first, xblk, zero), jnp.where(first, zero, xblk)], axis=0)
                    yd = _mm(jnp.concatenate(ms, axis=1), x_bd)
                    e_f = jnp.exp2(jnp.where(first, col[0], col[1]))
                    e_b = jnp.exp2(jnp.where(first, col[2], col[3]))
                    sl = slice(j * LANES, (j + 1) * LANES)
                    blocks.append(yd + y_f[:, sl] * e_f + y_b[:, sl] * e_b + xblk.astype(F32) * dskip[:, sl])
                    dec.append(e_f[CHUNK - 1:CHUNK, :])
                s["y"][rows(c), :] = jnp.concatenate(blocks, axis=1).astype(BF16)
                dec = jnp.concatenate(dec, axis=1)
            else:
                last16 = pl.ds(pl.multiple_of(c * CHUNK + CHUNK - 16, 16), 16)
                dec = _mm(s["e"][last16, :], expf[...])[15:16, :]
            xw = (xc.astype(F32) * _mm(s["w"][rows(c), :], expf[...])).astype(BF16)
            st_f[...] = st_f[...] * dec + _mm(bt, xw)
            return carry

        lax.fori_loop(0, n_chunks, body, 0, unroll=2)

    st_b[...] = jnp.zeros_like(st_b)
    bwd_pass(ctx)
    bwd_pass(lat)
    st_f[...] = jnp.zeros_like(st_f)
    fwd_pass(ctx)
    fwd_pass(lat)
    if not with_ctx_out:
        y_c[...] = jnp.zeros_like(y_c)


def _head_select_tables():
    g = jnp.arange(SSD_GROUPS)[:, None, None]
    k = jnp.arange(2 * LANES)[None, :, None]
    n = jnp.arange(2 * HEADS_PER_GROUP * CHUNK)[None, None, :]
    q = n // CHUNK
    head_dir = (q % 4 // 2) * HEADS_PER_GROUP + 2 * (q // 4) + q % 2
    selcs = ((k % PIECE) == 2 * HEADS_PER_GROUP * g + head_dir) & (k < 3 * PIECE)
    k2 = jnp.arange(LANES)[None, :, None]
    n2 = jnp.arange(GROUP_DIM)[None, None, :]
    expf = (k2 % PIECE) == 2 * HEADS_PER_GROUP * g + n2 // SSD_HEAD_DIM
    expb = (k2 % PIECE) == 2 * HEADS_PER_GROUP * g + HEADS_PER_GROUP + n2 // SSD_HEAD_DIM
    return selcs.astype(BF16), expf.astype(BF16), expb.astype(BF16)


def ssd(conv_in_l, tabs_l, conv_in_c, tabs_c, conv_w, conv_b, dskip_row, tables, with_ctx_out):
    b, _, l, _ = conv_in_l[0].shape
    lc = conv_in_c[0].shape[2] // b
    n_bg = D_INNER // D_STATE
    slab = lambda b, g: (b, g, 0, 0)
    slab_c = lambda b, g: (0, g, b, 0)

    def seq_specs(n, slab):
        return [pl.BlockSpec((None, None, n, GROUP_DIM), slab),
                pl.BlockSpec((None, None, n, D_STATE), slab),
                pl.BlockSpec((None, None, n, D_STATE), slab),
                pl.BlockSpec((None, n, 2 * LANES), lambda b, g: (b, 0, 0)),
                pl.BlockSpec((None, n, LANES), lambda b, g: (b, 0, 0)),
                pl.BlockSpec((None, n, LANES), lambda b, g: (b, 0, 0)),
                pl.BlockSpec((None, n // CHUNK, 2 * HEADS_PER_GROUP, CHUNK), lambda b, g: (b, 0, g, 0))]

    def seq_scratch(n):
        return [pltpu.VMEM((n, GROUP_DIM), BF16), pltpu.VMEM((n // CHUNK, D_STATE, CHUNK), BF16),
                pltpu.VMEM((n, D_STATE), BF16), pltpu.VMEM((n // CHUNK, D_STATE, GROUP_DIM), BF16)]

    par_specs = [pl.BlockSpec((3, GROUP_DIM), lambda b, g: (0, g)),
                 pl.BlockSpec((3, D_STATE), lambda b, g: (0, n_bg + g)),
                 pl.BlockSpec((3, D_STATE), lambda b, g: (0, n_bg + SSD_GROUPS + g)),
                 pl.BlockSpec((1, GROUP_DIM), lambda b, g: (0, g)),
                 pl.BlockSpec((1, D_STATE), lambda b, g: (0, n_bg + g)),
                 pl.BlockSpec((1, D_STATE), lambda b, g: (0, n_bg + SSD_GROUPS + g)),
                 pl.BlockSpec((1, GROUP_DIM), lambda b, g: (0, g)),
                 pl.BlockSpec((None, 2 * LANES, 2 * HEADS_PER_GROUP * CHUNK), lambda b, g: (g, 0, 0)),
                 pl.BlockSpec((None, LANES, GROUP_DIM), lambda b, g: (g, 0, 0)),
                 pl.BlockSpec((None, LANES, GROUP_DIM), lambda b, g: (g, 0, 0))]
    return pl.pallas_call(
        functools.partial(_ssd_kernel, with_ctx_out),
        grid=(b, SSD_GROUPS),
        in_specs=seq_specs(l, slab) + seq_specs(lc, slab_c) + par_specs,
        out_specs=[pl.BlockSpec((None, None, l, GROUP_DIM), slab),
                   pl.BlockSpec((None, None, lc, GROUP_DIM), slab_c)],
        out_shape=[jax.ShapeDtypeStruct((b, SSD_GROUPS, l, GROUP_DIM), BF16),
                   jax.ShapeDtypeStruct((1, SSD_GROUPS, b * lc, GROUP_DIM), BF16)],
        scratch_shapes=seq_scratch(l) + seq_scratch(lc) + [pltpu.VMEM((D_STATE, GROUP_DIM), F32)] * 2,
        compiler_params=_cparams(("arbitrary", "arbitrary")),
        name="ssd",
    )(*conv_in_l, *tabs_l, *conv_in_c, *tabs_c,
      conv_w, conv_w, conv_w, conv_b, conv_b, conv_b, dskip_row, *tables)


def _attn_kernel(q_ref, k_ref, v_ref, o_ref):
    def scores(h):
        kh = h // Q_PER_KV
        return lax.dot_general(q_ref[:, h * ATT_HEAD_DIM:(h + 1) * ATT_HEAD_DIM],
                               k_ref[:, kh * ATT_HEAD_DIM:(kh + 1) * ATT_HEAD_DIM],
                               (((1,), (1,)), ((), ())), preferred_element_type=F32)

    def finish(h, s):
        kh = h // Q_PER_KV
        p = jnp.exp2(s - jnp.max(s, axis=-1, keepdims=True)).astype(BF16)
        o = _mm(p, v_ref[:, 2 * kh * ATT_HEAD_DIM:2 * (kh + 1) * ATT_HEAD_DIM])
        o_ref[:, h * ATT_HEAD_DIM:(h + 1) * ATT_HEAD_DIM] = (
            o[:, :ATT_HEAD_DIM] / o[:, ATT_HEAD_DIM:ATT_HEAD_DIM + 1]).astype(BF16)

    group = 4
    for h0 in range(0, N_Q_HEADS, group):
        ss = [scores(h) for h in range(h0, h0 + group)]
        for h, s in zip(range(h0, h0 + group), ss):
            finish(h, s)


def attention(q, k_all, v_ones, tq):
    b, l, _ = q.shape
    nk = k_all.shape[1]
    return pl.pallas_call(
        _attn_kernel,
        grid=(b, l // tq),
        in_specs=[pl.BlockSpec((None, tq, ATT_DIM), lambda b, i: (b, i, 0)),
                  pl.BlockSpec((None, nk, KV_DIM), lambda b, i: (b, 0, 0)),
                  pl.BlockSpec((None, nk, 2 * KV_DIM), lambda b, i: (b, 0, 0))],
        out_specs=pl.BlockSpec((None, tq, ATT_DIM), lambda b, i: (b, i, 0)),
        out_shape=jax.ShapeDtypeStruct((b, l, ATT_DIM), BF16),
        compiler_params=_cparams(("arbitrary", "arbitrary")),
        name="attention",
    )(q, k_all, v_ones)


def _merge_kernel(y_ref, z_ref, sng_ref, att_ref, gs_ref, ga_ref, x_ref, g2_ref, wso_ref, wao_ref, wo_ref, o_ref):
    half = x_ref.shape[0] // 2
    for r in (slice(0, half), slice(half, 2 * half)):
        m = _sigmoid(ga_ref[r, :].astype(F32)) * _mm(att_ref[r, :], wao_ref[...])
        y_ssd = None
        for g in range(SSD_GROUPS):
            sl = slice(g * GROUP_DIM, (g + 1) * GROUP_DIM)
            yg = y_ref[g, r, :].astype(F32) * _silu(z_ref[r, sl].astype(F32))
            yn = yg * lax.rsqrt(jnp.mean(yg * yg, axis=-1, keepdims=True) + EPS) * sng_ref[:, sl]
            part = _mm(yn.astype(BF16), wso_ref[sl, :])
            y_ssd = part if y_ssd is None else y_ssd + part
        m = m + _sigmoid(gs_ref[r, :].astype(F32)) * y_ssd
        u = _mm(m.astype(BF16), wo_ref[...])
        o_ref[r, :] = x_ref[r, :] + g2_ref[...] * u


def merge(y, z, sng, att, gs, ga, x, mods, row_of_b, wso, wao, wo, tm):
    b, l, d = x.shape
    tok = lambda b, i: (b, i, 0)
    const = lambda b, i: (0, 0)
    return pl.pallas_call(
        _merge_kernel,
        grid=(b, l // tm),
        in_specs=[pl.BlockSpec((None, SSD_GROUPS, tm, GROUP_DIM), lambda b, i: (b, 0, i, 0)),
                  pl.BlockSpec((None, tm, D_INNER), tok),
                  pl.BlockSpec((1, D_INNER), const), pl.BlockSpec((None, tm, ATT_DIM), tok),
                  pl.BlockSpec((None, tm, d), tok), pl.BlockSpec((None, tm, d), tok),
                  pl.BlockSpec((None, tm, d), tok), _mod_spec(row_of_b, 5),
                  pl.BlockSpec((D_INNER, d), const), pl.BlockSpec((ATT_DIM, d), const),
                  pl.BlockSpec((d, d), const)],
        out_specs=pl.BlockSpec((None, tm, d), tok),
        out_shape=jax.ShapeDtypeStruct(x.shape, F32),
        compiler_params=_cparams(("arbitrary", "arbitrary")),
        name="merge",
    )(y, z, sng, att, gs, ga, x, mods, wso, wao, wo)


def _rope_tables(n_tok):
    rows = n_tok // GRID_W
    row = jnp.broadcast_to(jnp.arange(rows)[:, None], (rows, GRID_W)).reshape(-1).astype(F32)
    col = jnp.broadcast_to(jnp.arange(GRID_W)[None, :], (rows, GRID_W)).reshape(-1).astype(F32)
    inv_freq = ROPE_THETA ** (-(jnp.arange(0, ROPE_AXIS_DIM, 2, dtype=F32) / ROPE_AXIS_DIM))
    ar = row[:, None] * inv_freq
    ac = col[:, None] * inv_freq
    cos_t = jnp.concatenate([jnp.cos(ar), jnp.cos(ar), jnp.cos(ac), jnp.cos(ac)], axis=1)
    sin_t = jnp.concatenate([-jnp.sin(ar), jnp.sin(ar), -jnp.sin(ac), jnp.sin(ac)], axis=1)
    return cos_t, sin_t


def _dt_lanes(fwd, bwd):
    lead = fwd.shape[:-1]
    f = fwd.reshape(*lead, SSD_GROUPS, 1, HEADS_PER_GROUP)
    bk = bwd.reshape(*lead, SSD_GROUPS, 1, HEADS_PER_GROUP)
    half = jnp.concatenate([f, bk], axis=-2).reshape(*lead, PIECE)
    return jnp.concatenate([half, half], axis=-1)


def _permute_w_in(w_in):
    o = D_INNER + CONV_DIM
    dt_cols = _dt_lanes(w_in[..., o:o + SSD_HEADS], w_in[..., o + SSD_HEADS:o + 2 * SSD_HEADS])
    return jnp.concatenate([w_in[..., :o], w_in[..., o + 2 * SSD_HEADS:], dt_cols], axis=-1).astype(BF16)


def kernel(x, c, ctx, c_ctx, w_ada, b_ada, norm_g, ffn1_w13, ffn1_w2, w_in, conv_w, conv_b,
           a_log, dt_bias, d_skip, ssd_norm_g, w_ssd_out, qk_norm_g, w_attn_out, w_out,
           ffn2_w13, ffn2_w2):
    batch, seq, d = x.shape
    lc = ctx.shape[1]
    tm_l = tm_c = 512
    assert batch + 1 <= MOD_ROWS and seq % tm_l == 0 and lc % CHUNK == 0 and (batch * lc) % tm_c == 0

    cond = jnp.zeros((MOD_ROWS, d), F32).at[:batch].set(c).at[batch].set(c_ctx)
    mods_all = ada_all(cond, w_ada, b_ada).reshape(DEPTH, MOD_ROWS, 1, N_MOD * d)

    cos_l, sin_l = _rope_tables(seq)
    cos_c = jnp.ones((batch * lc, ATT_HEAD_DIM), F32)
    sin_c = jnp.zeros((batch * lc, ATT_HEAD_DIM), F32)

    w_in_p = _permute_w_in(w_in)
    f1_w13, f1_w2 = ffn1_w13.astype(BF16), ffn1_w2.astype(BF16)
    f2_w13, f2_w2 = ffn2_w13.astype(BF16), ffn2_w2.astype(BF16)
    wso, wao, wo = w_ssd_out.astype(BF16), w_attn_out.astype(BF16), w_out.astype(BF16)
    dtb_rows = _dt_lanes(dt_bias[:, 0], dt_bias[:, 1]).reshape(DEPTH, 1, LANES)
    nega = -jnp.exp(a_log.astype(F32))
    nega_rows = _dt_lanes(nega[:, 0], nega[:, 1]).reshape(DEPTH, 1, LANES)
    dskip_rows = jnp.repeat(d_skip, SSD_HEAD_DIM, axis=-1).reshape(DEPTH, 1, D_INNER)
    tables = _head_select_tables()

    lat_row = lambda b: b
    ctx_row = lambda b: batch
    xc = ctx.reshape(1, batch * lc, d)
    per_batch = lambda a: a.reshape(batch, lc, a.shape[-1])
    for i in range(DEPTH):
        last = i == DEPTH - 1
        mods = mods_all[i]
        ng = norm_g[i]
        sng = ssd_norm_g[i].reshape(1, D_INNER)
        x = ffn(x, mods, lat_row, 0, ng[0:1], f1_w13[i], f1_w2[i], tm_l)
        xc = ffn(xc, mods, ctx_row, 0, ng[0:1], f1_w13[i], f1_w2[i], tm_c)

        z, q, k, v, gs, ga, dt, *conv_in = inproj(x, mods, lat_row, ng[1:2], w_in_p[i], qk_norm_g[i], cos_l, sin_l, tm_l)
        zc, qc, kc, vc, gsc, gac, dtc, *conv_in_c = inproj(xc, mods, ctx_row, ng[1:2], w_in_p[i], qk_norm_g[i],
                                                           cos_c, sin_c, tm_c)
        tabs = dtprep(dt, dtb_rows[i], nega_rows[i])
        tabs_c = dtprep(per_batch(dtc), dtb_rows[i], nega_rows[i])
        y, yc = ssd(conv_in, tabs, conv_in_c, tabs_c, conv_w[i], conv_b[i].reshape(1, CONV_DIM), dskip_rows[i],
                    tables, not last)
        kc, vc = per_batch(kc), per_batch(vc)
        att = attention(q, jnp.concatenate([kc, k], axis=1), jnp.concatenate([vc, v], axis=1), 256)
        x = merge(y, z, sng, att, gs, ga, x, mods, lat_row, wso[i], wao[i], wo[i], tm_l)
        x = ffn(x, mods, lat_row, 6, ng[2:3], f2_w13[i], f2_w2[i], tm_l)
        if not last:
            attc = attention(per_batch(qc), kc, vc, lc).reshape(1, batch * lc, ATT_DIM)
            xc = merge(yc, zc, sng, attc, gsc, gac, xc, mods, ctx_row, wso[i], wao[i], wo[i], tm_c)
            xc = ffn(xc, mods, ctx_row, 6, ng[2:3], f2_w13[i], f2_w2[i], tm_c)
    return x
```

```python
import functools
import math

import jax
import jax.numpy as jnp
from jax import lax
from jax.experimental import pallas as pl
from jax.experimental.pallas import tpu as pltpu

F32 = jnp.float32
BF16 = jnp.bfloat16

D_MODEL = 1024
DEPTH = 4
GRID_W = 64
EPS = 1e-6
N_MOD = 9
D_FF = 2816

D_INNER = 2048
SSD_HEAD_DIM = 64
SSD_HEADS = 32
SSD_GROUPS = 4
HEADS_PER_GROUP = SSD_HEADS // SSD_GROUPS
GROUP_DIM = HEADS_PER_GROUP * SSD_HEAD_DIM
D_STATE = 128
BC_DIM = SSD_GROUPS * D_STATE
CONV_DIM = D_INNER + 2 * BC_DIM
CHUNK = 128

ATT_HEAD_DIM = 128
N_Q_HEADS = 8
N_KV_HEADS = 2
Q_PER_KV = N_Q_HEADS // N_KV_HEADS
ATT_DIM = N_Q_HEADS * ATT_HEAD_DIM
KV_DIM = N_KV_HEADS * ATT_HEAD_DIM
ATT_SCALE = ATT_HEAD_DIM ** -0.5
ROPE_THETA = 10000.0
ROPE_AXIS_DIM = ATT_HEAD_DIM // 2

LANES = 128
PIECE = 2 * SSD_HEADS
LOG2E = math.log2(math.e)
OFF_Z = 0
OFF_XBC = OFF_Z + D_INNER
OFF_Q = OFF_XBC + CONV_DIM
OFF_K = OFF_Q + ATT_DIM
OFF_V = OFF_K + KV_DIM
OFF_GS = OFF_V + KV_DIM
OFF_GA = OFF_GS + D_MODEL
OFF_DT = OFF_GA + D_MODEL
IN_COLS = OFF_DT + LANES

MOD_ROWS = 16
VMEM_LIMIT = 56 * 1024 * 1024


def _cparams(sem):
    return pltpu.CompilerParams(dimension_semantics=sem, vmem_limit_bytes=VMEM_LIMIT)


def _sigmoid(v):
    return 0.5 + 0.5 * jnp.tanh(0.5 * v)


def _silu(v):
    h = 0.5 * v
    return h + h * jnp.tanh(h)


def _norm_mod(x, ng, sc, sh):
    r = lax.rsqrt(jnp.mean(x * x, axis=-1, keepdims=True) + EPS)
    return (x * r * ng) * (1.0 + sc) + sh


def _mm(a, b):
    return jnp.dot(a, b, preferred_element_type=F32)


def _ada_kernel(cond_ref, w_ref, b_ref, o_ref):
    a = _silu(cond_ref[...])
    o_ref[...] = jnp.dot(a, w_ref[...], preferred_element_type=F32,
                         precision=lax.Precision.HIGHEST) + b_ref[...]


def ada_all(cond, w_ada, b_ada, tn=1024):
    depth, d, n = w_ada.shape
    return pl.pallas_call(
        _ada_kernel,
        grid=(depth, n // tn),
        in_specs=[pl.BlockSpec((MOD_ROWS, d), lambda l, j: (0, 0)),
                  pl.BlockSpec((None, d, tn), lambda l, j: (l, 0, j)),
                  pl.BlockSpec((None, 1, tn), lambda l, j: (l, 0, j))],
        out_specs=pl.BlockSpec((None, MOD_ROWS, tn), lambda l, j: (l, 0, j)),
        out_shape=jax.ShapeDtypeStruct((depth, MOD_ROWS, n), F32),
        compiler_params=_cparams(("arbitrary", "arbitrary")),
        name="ada_params",
    )(cond, w_ada, b_ada.reshape(depth, 1, n))


def _mod_spec(layer, row_of_b, k):
    return pl.BlockSpec((None, None, 1, D_MODEL), lambda b, i, *_: (layer, row_of_b(b), 0, k))


def _layer_spec(layer, *block):
    return pl.BlockSpec((None, *block), lambda *_: (layer,) + (0,) * len(block))


FF_STEP = 2816


def _ffn_kernel(x_ref, sh_ref, sc_ref, gt_ref, ng_ref, w13_ref, w2_ref, o_ref):
    x = x_ref[...]
    h = _norm_mod(x, ng_ref[...], sc_ref[...], sh_ref[...]).astype(BF16)
    acc = None
    for c in range(0, D_FF, FF_STEP):
        a = _mm(h, w13_ref[:, c:c + FF_STEP])
        g = _mm(h, w13_ref[:, D_FF + c:D_FF + c + FF_STEP])
        part = _mm((_silu(a) * g).astype(BF16), w2_ref[c:c + FF_STEP, :])
        acc = part if acc is None else acc + part
    o_ref[...] = x + 0.5 * gt_ref[...] * acc


def ffn(x, layer, mods, row_of_b, k0, norm_g, w13, w2, tm):
    b, l, d = x.shape
    tok = lambda b, i: (b, i, 0)
    return pl.pallas_call(
        _ffn_kernel,
        grid=(b, l // tm),
        in_specs=[pl.BlockSpec((None, tm, d), tok),
                  _mod_spec(layer, row_of_b, k0), _mod_spec(layer, row_of_b, k0 + 1),
                  _mod_spec(layer, row_of_b, k0 + 2),
                  pl.BlockSpec((None, None, 1, d), lambda b, i: (layer, k0 // 3, 0, 0)),
                  pl.BlockSpec((None, d, 2 * D_FF), lambda b, i: (layer, 0, 0), pipeline_mode=pl.Buffered(1)),
                  pl.BlockSpec((None, D_FF, d), lambda b, i: (layer, 0, 0), pipeline_mode=pl.Buffered(1))],
        out_specs=pl.BlockSpec((None, tm, d), tok),
        out_shape=jax.ShapeDtypeStruct(x.shape, F32),
        compiler_params=_cparams(("arbitrary", "arbitrary")),
        name="ffn",
    )(x, mods, mods, mods, norm_g, w13, w2)


def _swap32(v):
    lane = lax.broadcasted_iota(jnp.int32, v.shape, 1)
    return jnp.where((lane & 32) == 0, pltpu.roll(v, LANES - 32, axis=1), pltpu.roll(v, 32, axis=1))


def _inproj_kernel(x_ref, sh_ref, sc_ref, ng_ref, w_ref, qkg_ref, cos_ref, sin_ref,
                   z_ref, q_ref, k_ref, v_ref, gs_ref, ga_ref, dt_ref, xs_ref, bm_ref, cm_ref):
    h = _norm_mod(x_ref[...], ng_ref[...], sc_ref[...], sh_ref[...]).astype(BF16)

    def proj(lo, hi):
        return _mm(h, w_ref[:, lo:hi])

    def proj_to(o_ref, lo, width, step=1024):
        for c in range(0, width, step):
            o_ref[:, c:c + step] = proj(lo + c, lo + c + step).astype(BF16)

    cos = cos_ref[...]
    sin = sin_ref[...]

    def heads_to(o_ref, lo, n_heads, gain, scale):
        for h0 in range(0, n_heads, 2):
            pair = proj(lo + h0 * ATT_HEAD_DIM, lo + (h0 + 2) * ATT_HEAD_DIM)
            for hh in (h0, h0 + 1):
                t = pair[:, (hh - h0) * ATT_HEAD_DIM:(hh - h0 + 1) * ATT_HEAD_DIM]
                t = t * lax.rsqrt(jnp.mean(t * t, axis=-1, keepdims=True) + EPS) * gain
                t = t * cos + _swap32(t) * sin
                t = t * scale if scale != 1.0 else t
                o_ref[:, hh * ATT_HEAD_DIM:(hh + 1) * ATT_HEAD_DIM] = t.astype(BF16)

    heads_to(q_ref, OFF_Q, N_Q_HEADS, qkg_ref[0:1, :], ATT_SCALE * LOG2E)
    heads_to(k_ref, OFF_K, N_KV_HEADS, qkg_ref[1:2, :], 1.0)

    proj_to(z_ref, OFF_Z, D_INNER)
    for g in range(SSD_GROUPS):
        xs_ref[g] = proj(OFF_XBC + g * GROUP_DIM, OFF_XBC + (g + 1) * GROUP_DIM).astype(BF16)
    bc = proj(OFF_XBC + D_INNER, OFF_Q).astype(BF16)
    for g in range(SSD_GROUPS):
        bm_ref[g] = bc[:, g * D_STATE:(g + 1) * D_STATE]
        cm_ref[g] = bc[:, BC_DIM + g * D_STATE:BC_DIM + (g + 1) * D_STATE]
    vv = proj(OFF_V, OFF_GS).astype(BF16)
    for hh in range(N_KV_HEADS):
        lo = 2 * hh * ATT_HEAD_DIM
        v_ref[:, lo:lo + ATT_HEAD_DIM] = vv[:, hh * ATT_HEAD_DIM:(hh + 1) * ATT_HEAD_DIM]
        v_ref[:, lo + ATT_HEAD_DIM:lo + 2 * ATT_HEAD_DIM] = jnp.ones((x_ref.shape[0], ATT_HEAD_DIM), BF16)
    proj_to(gs_ref, OFF_GS, D_MODEL)
    proj_to(ga_ref, OFF_GA, D_MODEL)
    dt_ref[...] = proj(OFF_DT, IN_COLS)


def inproj(x, layer, mods, row_of_b, norm_g, w_in, qk_g, cos_t, sin_t, tm):
    b, l, d = x.shape
    tok = lambda b, i: (b, i, 0)
    widths = (D_INNER, ATT_DIM, KV_DIM, 2 * KV_DIM, D_MODEL, D_MODEL, LANES)
    dtypes = (BF16, BF16, BF16, BF16, BF16, BF16, F32)
    g_widths = (GROUP_DIM, D_STATE, D_STATE)
    return pl.pallas_call(
        _inproj_kernel,
        grid=(b, l // tm),
        in_specs=[pl.BlockSpec((None, tm, d), tok),
                  _mod_spec(layer, row_of_b, 3), _mod_spec(layer, row_of_b, 4),
                  pl.BlockSpec((None, None, 1, d), lambda b, i: (layer, 1, 0, 0)),
                  pl.BlockSpec((None, d, IN_COLS), lambda b, i: (layer, 0, 0), pipeline_mode=pl.Buffered(1)),
                  _layer_spec(layer, 2, ATT_HEAD_DIM),
                  pl.BlockSpec((tm, ATT_HEAD_DIM), lambda b, i: (i, 0)),
                  pl.BlockSpec((tm, ATT_HEAD_DIM), lambda b, i: (i, 0))],
        out_specs=[pl.BlockSpec((None, tm, w), tok) for w in widths]
        + [pl.BlockSpec((None, SSD_GROUPS, tm, w), lambda b, i: (b, 0, i, 0)) for w in g_widths],
        out_shape=[jax.ShapeDtypeStruct((b, l, w), dt) for w, dt in zip(widths, dtypes)]
        + [jax.ShapeDtypeStruct((b, SSD_GROUPS, l, w), BF16) for w in g_widths],
        compiler_params=_cparams(("arbitrary", "arbitrary")),
        name="inproj",
    )(x, mods, mods, norm_g, w_in, qk_g, cos_t, sin_t)


def _scan_rows(v, suffix):
    n = v.shape[0]
    row = lax.broadcasted_iota(jnp.int32, v.shape, 0)
    k = 1
    while k < n:
        if suffix:
            v = v + jnp.where(row < n - k, pltpu.roll(v, n - k, axis=0), 0.0)
        else:
            v = v + jnp.where(row >= k, pltpu.roll(v, k, axis=0), 0.0)
        k *= 2
    return v


def _split_bf16(v, low, pieces):
    hi = v.astype(BF16).astype(F32)
    r1 = v - hi
    mid = r1.astype(BF16).astype(F32)
    first = jnp.where(low, hi, mid).astype(BF16)
    if pieces == 2:
        return first
    return jnp.concatenate([first, jnp.where(low, r1 - mid, 0.0).astype(BF16)], axis=1)


def _dtprep_kernel(raw_ref, bias_ref, nega_ref, cs_ref, e_ref, w_ref, rt_ref):
    n_chunks = raw_ref.shape[0] // CHUNK
    lane = lax.broadcasted_iota(jnp.int32, (CHUNK, LANES), 1)
    is_bwd = (lane & HEADS_PER_GROUP) != 0
    low = lane < PIECE
    for c in range(n_chunks):
        rows = slice(c * CHUNK, (c + 1) * CHUNK)
        v = raw_ref[rows, :] + bias_ref[...]
        dt = jnp.maximum(v, 0.0) + jnp.log1p(jnp.exp(-jnp.abs(v)))
        da = dt * nega_ref[...]
        pre = _scan_rows(da, suffix=False)
        cs = jnp.where(is_bwd, _scan_rows(da, suffix=True), pre)
        tot = pre[CHUNK - 1:CHUNK, :]
        cs_ref[rows, :] = _split_bf16(cs * LOG2E, low, 3)
        e_ref[rows, :] = _split_bf16(jnp.exp(cs), low, 2)
        w_ref[rows, :] = _split_bf16(dt * jnp.exp(tot - cs), low, 2)
        rt_ref[c] = ((cs - jnp.log(dt)) * LOG2E).T[0:PIECE, :]


def dtprep(raw, layer, bias_rows, nega_rows):
    b, l, _ = raw.shape
    r = min(l, 1024)
    tok = lambda b, i: (b, i, 0)
    return pl.pallas_call(
        _dtprep_kernel,
        grid=(b, l // r),
        in_specs=[pl.BlockSpec((None, r, LANES), tok), _layer_spec(layer, 1, LANES), _layer_spec(layer, 1, LANES)],
        out_specs=[pl.BlockSpec((None, r, 2 * LANES), tok), pl.BlockSpec((None, r, LANES), tok),
                   pl.BlockSpec((None, r, LANES), tok),
                   pl.BlockSpec((None, r // CHUNK, PIECE, CHUNK), lambda b, i: (b, i, 0, 0))],
        out_shape=[jax.ShapeDtypeStruct((b, l, 2 * LANES), BF16), jax.ShapeDtypeStruct((b, l, LANES), BF16),
                   jax.ShapeDtypeStruct((b, l, LANES), BF16),
                   jax.ShapeDtypeStruct((b, l // CHUNK, PIECE, CHUNK), F32)],
        compiler_params=_cparams(("arbitrary", "arbitrary")),
        name="dtprep",
    )(raw, bias_rows, nega_rows)


def _ssd_kernel(with_ctx_out,
                xs_l, bm_l, cm_l, cs_l, e_l, w_l, rt_l, xs_c, bm_c, cm_c, cs_c, e_c, w_c, rt_c,
                cwx, cwb, cwc, cbx, cbb, cbc, dskip, selcs, expf, expb,
                y_l, y_c,
                xc_l, bt_l, cc_l, sb_l, xc_c, bt_c, cc_c, sb_c, st_f, st_b):
    lat = dict(xs=xs_l, bm=bm_l, cm=cm_l, cs=cs_l, e=e_l, w=w_l, rt=rt_l, y=y_l,
               xc=xc_l, bt=bt_l, cc=cc_l, sb=sb_l, out=True)
    ctx = dict(xs=xs_c, bm=bm_c, cm=cm_c, cs=cs_c, e=e_c, w=w_c, rt=rt_c, y=y_c,
               xc=xc_c, bt=bt_c, cc=cc_c, sb=sb_c, out=with_ctx_out)

    def rows(c):
        return pl.ds(pl.multiple_of(c * CHUNK, CHUNK), CHUNK)

    def bwd_pass(s):
        n_chunks = s["xs"].shape[0] // CHUNK
        row = lax.broadcasted_iota(jnp.int32, (CHUNK, 1), 0)

        def conv(src, w_ref, b_ref, c):
            cur = src[rows(c), :].astype(F32)
            p_at = pl.multiple_of(jnp.maximum(c * CHUNK - 16, 0), 16)
            n_at = pl.multiple_of(jnp.minimum((c + 1) * CHUNK, n_chunks * CHUNK - 16), 16)
            prev = jnp.where(c > 0, src[pl.ds(p_at, 16), :].astype(F32)[15:16, :], 0.0)
            nxt = jnp.where(c < n_chunks - 1, src[pl.ds(n_at, 16), :].astype(F32)[0:1, :], 0.0)
            up = jnp.where(row == 0, prev, pltpu.roll(cur, 1, axis=0))
            dn = jnp.where(row == CHUNK - 1, nxt, pltpu.roll(cur, CHUNK - 1, axis=0))
            return _silu(w_ref[0:1, :] * up + w_ref[1:2, :] * cur + w_ref[2:3, :] * dn + b_ref[...])

        def body(i, carry):
            c = n_chunks - 1 - i
            xcv = conv(s["xs"], cwx, cbx, c)
            bt = conv(s["bm"], cwb, cbb, c).T.astype(BF16)
            s["xc"][rows(c), :] = xcv.astype(BF16)
            s["cc"][rows(c), :] = conv(s["cm"], cwc, cbc, c).astype(BF16)
            s["bt"][c] = bt
            s["sb"][c] = st_b[...].astype(BF16)
            w_b = _mm(s["w"][rows(c), :], expb[...])
            dec = _mm(s["e"][pl.ds(pl.multiple_of(c * CHUNK, CHUNK), 16), :], expb[...])[0:1, :]
            st_b[...] = st_b[...] * dec + _mm(bt, (xcv * w_b).astype(BF16))
            return carry

        lax.fori_loop(0, n_chunks, body, 0, unroll=2)

    def fwd_pass(s):
        n_chunks = s["xs"].shape[0] // CHUNK
        ri = lax.broadcasted_iota(jnp.int32, (CHUNK, CHUNK), 0)
        ci = lax.broadcasted_iota(jnp.int32, (CHUNK, CHUNK), 1)
        lower = ri >= ci
        upper = ri <= ci
        first = lax.broadcasted_iota(jnp.int32, (CHUNK, LANES), 1) < SSD_HEAD_DIM

        def body(c, carry):
            xc = s["xc"][rows(c), :]
            bt = s["bt"][c]
            if s["out"]:
                cc = s["cc"][rows(c), :]
                colb = _mm(s["cs"][rows(c), :], selcs[...])
                rt = s["rt"][c]
                cb = _mm(cc, bt)
                y_f = _mm(cc, st_f[...].astype(BF16))
                y_b = _mm(cc, s["sb"][c])
                blocks, dec = [], []
                for j in range(HEADS_PER_GROUP // 2):
                    col = [colb[:, (4 * j + t) * CHUNK:(4 * j + t + 1) * CHUNK] for t in range(4)]
                    ms = []
                    for t, hh in enumerate((2 * j, 2 * j + 1)):
                        hb = HEADS_PER_GROUP + hh
                        d_f = jnp.exp2(jnp.where(lower, col[t] - rt[hh:hh + 1, :], -jnp.inf))
                        d_b = jnp.exp2(jnp.where(upper, col[2 + t] - rt[hb:hb + 1, :], -jnp.inf))
                        ms.append((cb * (d_f + d_b)).astype(BF16))
                    xblk = xc[:, j * LANES:(j + 1) * LANES]
                    zero = jnp.zeros_like(xblk)
                    x_bd = jnp.concatenate([jnp.where(first, xblk, zero), jnp.where(first, zero, xblk)], axis=0)
                    yd = _mm(jnp.concatenate(ms, axis=1), x_bd)
                    e_f = jnp.exp2(jnp.where(first, col[0], col[1]))
                    e_b = jnp.exp2(jnp.where(first, col[2], col[3]))
                    sl = slice(j * LANES, (j + 1) * LANES)
                    blocks.append(yd + y_f[:, sl] * e_f + y_b[:, sl] * e_b + xblk.astype(F32) * dskip[:, sl])
                    dec.append(e_f[CHUNK - 1:CHUNK, :])
                s["y"][rows(c), :] = jnp.concatenate(blocks, axis=1).astype(BF16)
                dec = jnp.concatenate(dec, axis=1)
            else:
                last16 = pl.ds(pl.multiple_of(c * CHUNK + CHUNK - 16, 16), 16)
                dec = _mm(s["e"][last16, :], expf[...])[15:16, :]
            xw = (xc.astype(F32) * _mm(s["w"][rows(c), :], expf[...])).astype(BF16)
            st_f[...] = st_f[...] * dec + _mm(bt, xw)
            return carry

        lax.fori_loop(0, n_chunks, body, 0, unroll=2)

    st_b[...] = jnp.zeros_like(st_b)
    bwd_pass(ctx)
    bwd_pass(lat)
    st_f[...] = jnp.zeros_like(st_f)
    fwd_pass(ctx)
    fwd_pass(lat)
    if not with_ctx_out:
        y_c[...] = jnp.zeros_like(y_c)


def _head_select_tables():
    g = jnp.arange(SSD_GROUPS)[:, None, None]
    k = jnp.arange(2 * LANES)[None, :, None]
    n = jnp.arange(2 * HEADS_PER_GROUP * CHUNK)[None, None, :]
    q = n // CHUNK
    head_dir = (q % 4 // 2) * HEADS_PER_GROUP + 2 * (q // 4) + q % 2
    selcs = ((k % PIECE) == 2 * HEADS_PER_GROUP * g + head_dir) & (k < 3 * PIECE)
    k2 = jnp.arange(LANES)[None, :, None]
    n2 = jnp.arange(GROUP_DIM)[None, None, :]
    expf = (k2 % PIECE) == 2 * HEADS_PER_GROUP * g + n2 // SSD_HEAD_DIM
    expb = (k2 % PIECE) == 2 * HEADS_PER_GROUP * g + HEADS_PER_GROUP + n2 // SSD_HEAD_DIM
    return selcs.astype(BF16), expf.astype(BF16), expb.astype(BF16)


def ssd(conv_in_l, tabs_l, conv_in_c, tabs_c, layer, conv_w, conv_b, dskip_rows, tables, with_ctx_out):
    b, _, l, _ = conv_in_l[0].shape
    lc = conv_in_c[0].shape[2] // b
    n_bg = D_INNER // D_STATE
    slab = lambda b, g: (b, g, 0, 0)
    slab_c = lambda b, g: (0, g, b, 0)

    def seq_specs(n, slab):
        return [pl.BlockSpec((None, None, n, GROUP_DIM), slab),
                pl.BlockSpec((None, None, n, D_STATE), slab),
                pl.BlockSpec((None, None, n, D_STATE), slab),
                pl.BlockSpec((None, n, 2 * LANES), lambda b, g: (b, 0, 0)),
                pl.BlockSpec((None, n, LANES), lambda b, g: (b, 0, 0)),
                pl.BlockSpec((None, n, LANES), lambda b, g: (b, 0, 0)),
                pl.BlockSpec((None, n // CHUNK, 2 * HEADS_PER_GROUP, CHUNK), lambda b, g: (b, 0, g, 0))]

    def seq_scratch(n):
        return [pltpu.VMEM((n, GROUP_DIM), BF16), pltpu.VMEM((n // CHUNK, D_STATE, CHUNK), BF16),
                pltpu.VMEM((n, D_STATE), BF16), pltpu.VMEM((n // CHUNK, D_STATE, GROUP_DIM), BF16)]

    par_specs = [pl.BlockSpec((None, 3, GROUP_DIM), lambda b, g: (layer, 0, g)),
                 pl.BlockSpec((None, 3, D_STATE), lambda b, g: (layer, 0, n_bg + g)),
                 pl.BlockSpec((None, 3, D_STATE), lambda b, g: (layer, 0, n_bg + SSD_GROUPS + g)),
                 pl.BlockSpec((None, 1, GROUP_DIM), lambda b, g: (layer, 0, g)),
                 pl.BlockSpec((None, 1, D_STATE), lambda b, g: (layer, 0, n_bg + g)),
                 pl.BlockSpec((None, 1, D_STATE), lambda b, g: (layer, 0, n_bg + SSD_GROUPS + g)),
                 pl.BlockSpec((None, 1, GROUP_DIM), lambda b, g: (layer, 0, g)),
                 pl.BlockSpec((None, 2 * LANES, 2 * HEADS_PER_GROUP * CHUNK), lambda b, g: (g, 0, 0)),
                 pl.BlockSpec((None, LANES, GROUP_DIM), lambda b, g: (g, 0, 0)),
                 pl.BlockSpec((None, LANES, GROUP_DIM), lambda b, g: (g, 0, 0))]
    return pl.pallas_call(
        functools.partial(_ssd_kernel, with_ctx_out),
        grid=(b, SSD_GROUPS),
        in_specs=seq_specs(l, slab) + seq_specs(lc, slab_c) + par_specs,
        out_specs=[pl.BlockSpec((None, None, l, GROUP_DIM), slab),
                   pl.BlockSpec((None, None, lc, GROUP_DIM), slab_c)],
        out_shape=[jax.ShapeDtypeStruct((b, SSD_GROUPS, l, GROUP_DIM), BF16),
                   jax.ShapeDtypeStruct((1, SSD_GROUPS, b * lc, GROUP_DIM), BF16)],
        scratch_shapes=seq_scratch(l) + seq_scratch(lc) + [pltpu.VMEM((D_STATE, GROUP_DIM), F32)] * 2,
        compiler_params=_cparams(("arbitrary", "arbitrary")),
        name="ssd",
    )(*conv_in_l, *tabs_l, *conv_in_c, *tabs_c,
      conv_w, conv_w, conv_w, conv_b, conv_b, conv_b, dskip_rows, *tables)


def _attn_kernel(q_ref, k_ref, v_ref, o_ref):
    def scores(h):
        kh = h // Q_PER_KV
        return lax.dot_general(q_ref[:, h * ATT_HEAD_DIM:(h + 1) * ATT_HEAD_DIM],
                               k_ref[:, kh * ATT_HEAD_DIM:(kh + 1) * ATT_HEAD_DIM],
                               (((1,), (1,)), ((), ())), preferred_element_type=F32)

    def finish(h, s):
        kh = h // Q_PER_KV
        p = jnp.exp2(s - jnp.max(s, axis=-1, keepdims=True)).astype(BF16)
        o = _mm(p, v_ref[:, 2 * kh * ATT_HEAD_DIM:2 * (kh + 1) * ATT_HEAD_DIM])
        o_ref[:, h * ATT_HEAD_DIM:(h + 1) * ATT_HEAD_DIM] = (
            o[:, :ATT_HEAD_DIM] / o[:, ATT_HEAD_DIM:ATT_HEAD_DIM + 1]).astype(BF16)

    group = 4
    for h0 in range(0, N_Q_HEADS, group):
        ss = [scores(h) for h in range(h0, h0 + group)]
        for h, s in zip(range(h0, h0 + group), ss):
            finish(h, s)


def attention(q, k_all, v_ones, tq):
    b, l, _ = q.shape
    nk = k_all.shape[1]
    return pl.pallas_call(
        _attn_kernel,
        grid=(b, l // tq),
        in_specs=[pl.BlockSpec((None, tq, ATT_DIM), lambda b, i: (b, i, 0)),
                  pl.BlockSpec((None, nk, KV_DIM), lambda b, i: (b, 0, 0)),
                  pl.BlockSpec((None, nk, 2 * KV_DIM), lambda b, i: (b, 0, 0))],
        out_specs=pl.BlockSpec((None, tq, ATT_DIM), lambda b, i: (b, i, 0)),
        out_shape=jax.ShapeDtypeStruct((b, l, ATT_DIM), BF16),
        compiler_params=_cparams(("arbitrary", "arbitrary")),
        name="attention",
    )(q, k_all, v_ones)


def _merge_kernel(y_ref, z_ref, sng_ref, att_ref, gs_ref, ga_ref, x_ref, g2_ref, wso_ref, wao_ref, wo_ref, o_ref):
    half = x_ref.shape[0] // 2
    for r in (slice(0, half), slice(half, 2 * half)):
        m = _sigmoid(ga_ref[r, :].astype(F32)) * _mm(att_ref[r, :], wao_ref[...])
        y_ssd = None
        for g in range(SSD_GROUPS):
            sl = slice(g * GROUP_DIM, (g + 1) * GROUP_DIM)
            yg = y_ref[g, r, :].astype(F32) * _silu(z_ref[r, sl].astype(F32))
            yn = yg * lax.rsqrt(jnp.mean(yg * yg, axis=-1, keepdims=True) + EPS) * sng_ref[:, sl]
            part = _mm(yn.astype(BF16), wso_ref[sl, :])
            y_ssd = part if y_ssd is None else y_ssd + part
        m = m + _sigmoid(gs_ref[r, :].astype(F32)) * y_ssd
        u = _mm(m.astype(BF16), wo_ref[...])
        o_ref[r, :] = x_ref[r, :] + g2_ref[...] * u


def merge(y, z, att, gs, ga, x, layer, mods, row_of_b, sng, wso, wao, wo, tm):
    b, l, d = x.shape
    tok = lambda b, i: (b, i, 0)
    return pl.pallas_call(
        _merge_kernel,
        grid=(b, l // tm),
        in_specs=[pl.BlockSpec((None, SSD_GROUPS, tm, GROUP_DIM), lambda b, i: (b, 0, i, 0)),
                  pl.BlockSpec((None, tm, D_INNER), tok),
                  _layer_spec(layer, 1, D_INNER), pl.BlockSpec((None, tm, ATT_DIM), tok),
                  pl.BlockSpec((None, tm, d), tok), pl.BlockSpec((None, tm, d), tok),
                  pl.BlockSpec((None, tm, d), tok), _mod_spec(layer, row_of_b, 5),
                  _layer_spec(layer, D_INNER, d), _layer_spec(layer, ATT_DIM, d), _layer_spec(layer, d, d)],
        out_specs=pl.BlockSpec((None, tm, d), tok),
        out_shape=jax.ShapeDtypeStruct(x.shape, F32),
        compiler_params=_cparams(("arbitrary", "arbitrary")),
        name="merge",
    )(y, z, sng, att, gs, ga, x, mods, wso, wao, wo)


def _rope_tables(n_tok):
    rows = n_tok // GRID_W
    row = jnp.broadcast_to(jnp.arange(rows)[:, None], (rows, GRID_W)).reshape(-1).astype(F32)
    col = jnp.broadcast_to(jnp.arange(GRID_W)[None, :], (rows, GRID_W)).reshape(-1).astype(F32)
    inv_freq = ROPE_THETA ** (-(jnp.arange(0, ROPE_AXIS_DIM, 2, dtype=F32) / ROPE_AXIS_DIM))
    ar = row[:, None] * inv_freq
    ac = col[:, None] * inv_freq
    cos_t = jnp.concatenate([jnp.cos(ar), jnp.cos(ar), jnp.cos(ac), jnp.cos(ac)], axis=1)
    sin_t = jnp.concatenate([-jnp.sin(ar), jnp.sin(ar), -jnp.sin(ac), jnp.sin(ac)], axis=1)
    return cos_t, sin_t


def _dt_lanes(fwd, bwd):
    lead = fwd.shape[:-1]
    f = fwd.reshape(*lead, SSD_GROUPS, 1, HEADS_PER_GROUP)
    bk = bwd.reshape(*lead, SSD_GROUPS, 1, HEADS_PER_GROUP)
    half = jnp.concatenate([f, bk], axis=-2).reshape(*lead, PIECE)
    return jnp.concatenate([half, half], axis=-1)


def _permute_w_in(w_in):
    o = D_INNER + CONV_DIM
    dt_cols = _dt_lanes(w_in[..., o:o + SSD_HEADS], w_in[..., o + SSD_HEADS:o + 2 * SSD_HEADS])
    return jnp.concatenate([w_in[..., :o], w_in[..., o + 2 * SSD_HEADS:], dt_cols], axis=-1).astype(BF16)


def kernel(x, c, ctx, c_ctx, w_ada, b_ada, norm_g, ffn1_w13, ffn1_w2, w_in, conv_w, conv_b,
           a_log, dt_bias, d_skip, ssd_norm_g, w_ssd_out, qk_norm_g, w_attn_out, w_out,
           ffn2_w13, ffn2_w2):
    batch, seq, d = x.shape
    lc = ctx.shape[1]
    tm_l = tm_c = 512
    assert batch + 1 <= MOD_ROWS and seq % tm_l == 0 and lc % CHUNK == 0 and (batch * lc) % tm_c == 0

    cond = jnp.zeros((MOD_ROWS, d), F32).at[:batch].set(c).at[batch].set(c_ctx)
    mods = ada_all(cond, w_ada, b_ada).reshape(DEPTH, MOD_ROWS, 1, N_MOD * d)

    cos_l, sin_l = _rope_tables(seq)
    cos_c = jnp.ones((batch * lc, ATT_HEAD_DIM), F32)
    sin_c = jnp.zeros((batch * lc, ATT_HEAD_DIM), F32)

    w_in_p = _permute_w_in(w_in)
    f1_w13, f1_w2 = ffn1_w13.astype(BF16), ffn1_w2.astype(BF16)
    f2_w13, f2_w2 = ffn2_w13.astype(BF16), ffn2_w2.astype(BF16)
    wso, wao, wo = w_ssd_out.astype(BF16), w_attn_out.astype(BF16), w_out.astype(BF16)
    dtb_rows = _dt_lanes(dt_bias[:, 0], dt_bias[:, 1]).reshape(DEPTH, 1, LANES)
    nega = -jnp.exp(a_log.astype(F32))
    nega_rows = _dt_lanes(nega[:, 0], nega[:, 1]).reshape(DEPTH, 1, LANES)
    dskip_rows = jnp.repeat(d_skip, SSD_HEAD_DIM, axis=-1).reshape(DEPTH, 1, D_INNER)
    norm_g = norm_g.reshape(DEPTH, 3, 1, d)
    sng = ssd_norm_g.reshape(DEPTH, 1, D_INNER)
    conv_b3 = conv_b.reshape(DEPTH, 1, CONV_DIM)
    tables = _head_select_tables()

    lat_row = lambda b: b
    ctx_row = lambda b: batch
    xc = ctx.reshape(1, batch * lc, d)
    per_batch = lambda a: a.reshape(batch, lc, a.shape[-1])
    for i in range(DEPTH):
        last = i == DEPTH - 1
        x = ffn(x, i, mods, lat_row, 0, norm_g, f1_w13, f1_w2, tm_l)
        xc = ffn(xc, i, mods, ctx_row, 0, norm_g, f1_w13, f1_w2, tm_c)

        z, q, k, v, gs, ga, dt, *conv_in = inproj(x, i, mods, lat_row, norm_g, w_in_p, qk_norm_g, cos_l, sin_l, tm_l)
        zc, qc, kc, vc, gsc, gac, dtc, *conv_in_c = inproj(xc, i, mods, ctx_row, norm_g, w_in_p, qk_norm_g,
                                                           cos_c, sin_c, tm_c)
        tabs = dtprep(dt, i, dtb_rows, nega_rows)
        tabs_c = dtprep(per_batch(dtc), i, dtb_rows, nega_rows)
        y, yc = ssd(conv_in, tabs, conv_in_c, tabs_c, i, conv_w, conv_b3, dskip_rows, tables, not last)
        kc, vc = per_batch(kc), per_batch(vc)
        att = attention(q, jnp.concatenate([kc, k], axis=1), jnp.concatenate([vc, v], axis=1), 256)
        x = merge(y, z, att, gs, ga, x, i, mods, lat_row, sng, wso, wao, wo, tm_l)
        x = ffn(x, i, mods, lat_row, 6, norm_g, f2_w13, f2_w2, tm_l)
        if not last:
            attc = attention(per_batch(qc), kc, vc, lc).reshape(1, batch * lc, ATT_DIM)
            xc = merge(yc, zc, attc, gsc, gac, xc, i, mods, ctx_row, sng, wso, wao, wo, tm_c)
            xc = ffn(xc, i, mods, ctx_row, 6, norm_g, f2_w13, f2_w2, tm_c)
    return x
```

```python
import functools
import math

import jax
import jax.numpy as jnp
from jax import lax
from jax.experimental import pallas as pl
from jax.experimental.pallas import tpu as pltpu

F32 = jnp.float32
BF16 = jnp.bfloat16

D_MODEL = 1024
DEPTH = 4
GRID_W = 64
EPS = 1e-6
N_MOD = 9
D_FF = 2816

D_INNER = 2048
SSD_HEAD_DIM = 64
SSD_HEADS = 32
SSD_GROUPS = 4
HEADS_PER_GROUP = SSD_HEADS // SSD_GROUPS
GROUP_DIM = HEADS_PER_GROUP * SSD_HEAD_DIM
D_STATE = 128
BC_DIM = SSD_GROUPS * D_STATE
CONV_DIM = D_INNER + 2 * BC_DIM
CHUNK = 128

ATT_HEAD_DIM = 128
N_Q_HEADS = 8
N_KV_HEADS = 2
Q_PER_KV = N_Q_HEADS // N_KV_HEADS
ATT_DIM = N_Q_HEADS * ATT_HEAD_DIM
KV_DIM = N_KV_HEADS * ATT_HEAD_DIM
ATT_SCALE = ATT_HEAD_DIM ** -0.5
ROPE_THETA = 10000.0
ROPE_AXIS_DIM = ATT_HEAD_DIM // 2

LANES = 128
PIECE = 2 * SSD_HEADS
LOG2E = math.log2(math.e)
OFF_Z = 0
OFF_XBC = OFF_Z + D_INNER
OFF_Q = OFF_XBC + CONV_DIM
OFF_K = OFF_Q + ATT_DIM
OFF_V = OFF_K + KV_DIM
OFF_GS = OFF_V + KV_DIM
OFF_GA = OFF_GS + D_MODEL
OFF_DT = OFF_GA + D_MODEL
IN_COLS = OFF_DT + LANES

MOD_ROWS = 16
VMEM_LIMIT = 56 * 1024 * 1024


def _cparams(sem):
    return pltpu.CompilerParams(dimension_semantics=sem, vmem_limit_bytes=VMEM_LIMIT)


def _sigmoid(v):
    return 0.5 + 0.5 * jnp.tanh(0.5 * v)


def _silu(v):
    h = 0.5 * v
    return h + h * jnp.tanh(h)


def _norm_mod(x, ng, sc, sh):
    r = lax.rsqrt(jnp.mean(x * x, axis=-1, keepdims=True) + EPS)
    return (x * r * ng) * (1.0 + sc) + sh


def _mm(a, b):
    return jnp.dot(a, b, preferred_element_type=F32)


def _ada_kernel(cond_ref, w_ref, b_ref, o_ref):
    a = _silu(cond_ref[...])
    o_ref[...] = jnp.dot(a, w_ref[...], preferred_element_type=F32,
                         precision=lax.Precision.HIGHEST) + b_ref[...]


def ada_all(cond, w_ada, b_ada, tn=1024):
    depth, d, n = w_ada.shape
    return pl.pallas_call(
        _ada_kernel,
        grid=(depth, n // tn),
        in_specs=[pl.BlockSpec((MOD_ROWS, d), lambda l, j: (0, 0)),
                  pl.BlockSpec((None, d, tn), lambda l, j: (l, 0, j)),
                  pl.BlockSpec((None, 1, tn), lambda l, j: (l, 0, j))],
        out_specs=pl.BlockSpec((None, MOD_ROWS, tn), lambda l, j: (l, 0, j)),
        out_shape=jax.ShapeDtypeStruct((depth, MOD_ROWS, n), F32),
        compiler_params=_cparams(("arbitrary", "arbitrary")),
        name="ada_params",
    )(cond, w_ada, b_ada.reshape(depth, 1, n))


def _mod_spec(layer, row_of_b, k):
    return pl.BlockSpec((None, None, 1, D_MODEL), lambda b, i, *_: (layer, row_of_b(b), 0, k))


def _layer_spec(layer, *block):
    return pl.BlockSpec((None, *block), lambda *_: (layer,) + (0,) * len(block))


FF_STEP = 2816


def _ffn_kernel(x_ref, sh_ref, sc_ref, gt_ref, ng_ref, w13_ref, w2_ref, o_ref):
    x = x_ref[...]
    h = _norm_mod(x, ng_ref[...], sc_ref[...], sh_ref[...]).astype(BF16)
    acc = None
    for c in range(0, D_FF, FF_STEP):
        a = _mm(h, w13_ref[:, c:c + FF_STEP])
        g = _mm(h, w13_ref[:, D_FF + c:D_FF + c + FF_STEP])
        part = _mm((_silu(a) * g).astype(BF16), w2_ref[c:c + FF_STEP, :])
        acc = part if acc is None else acc + part
    o_ref[...] = x + 0.5 * gt_ref[...] * acc


def ffn(x, layer, mods, row_of_b, k0, norm_g, w13, w2, tm):
    b, l, d = x.shape
    tok = lambda b, i: (b, i, 0)
    return pl.pallas_call(
        _ffn_kernel,
        grid=(b, l // tm),
        in_specs=[pl.BlockSpec((None, tm, d), tok),
                  _mod_spec(layer, row_of_b, k0), _mod_spec(layer, row_of_b, k0 + 1),
                  _mod_spec(layer, row_of_b, k0 + 2),
                  pl.BlockSpec((None, None, 1, d), lambda b, i: (layer, k0 // 3, 0, 0)),
                  pl.BlockSpec((None, d, 2 * D_FF), lambda b, i: (layer, 0, 0), pipeline_mode=pl.Buffered(1)),
                  pl.BlockSpec((None, D_FF, d), lambda b, i: (layer, 0, 0), pipeline_mode=pl.Buffered(1))],
        out_specs=pl.BlockSpec((None, tm, d), tok),
        out_shape=jax.ShapeDtypeStruct(x.shape, F32),
        compiler_params=_cparams(("arbitrary", "arbitrary")),
        name="ffn",
    )(x, mods, mods, mods, norm_g, w13, w2)


def _swap32(v):
    lane = lax.broadcasted_iota(jnp.int32, v.shape, 1)
    return jnp.where((lane & 32) == 0, pltpu.roll(v, LANES - 32, axis=1), pltpu.roll(v, 32, axis=1))


def _inproj_kernel(x_ref, sh_ref, sc_ref, ng_ref, w_ref, qkg_ref, cos_ref, sin_ref,
                   z_ref, q_ref, k_ref, v_ref, gs_ref, ga_ref, dt_ref, xs_ref, bm_ref, cm_ref):
    h = _norm_mod(x_ref[...], ng_ref[...], sc_ref[...], sh_ref[...]).astype(BF16)

    def proj(lo, hi):
        return _mm(h, w_ref[:, lo:hi])

    def proj_to(o_ref, lo, width, step=1024):
        for c in range(0, width, step):
            o_ref[:, c:c + step] = proj(lo + c, lo + c + step).astype(BF16)

    cos = cos_ref[...]
    sin = sin_ref[...]

    def heads_to(o_ref, lo, n_heads, gain, scale):
        for h0 in range(0, n_heads, 2):
            pair = proj(lo + h0 * ATT_HEAD_DIM, lo + (h0 + 2) * ATT_HEAD_DIM)
            for hh in (h0, h0 + 1):
                t = pair[:, (hh - h0) * ATT_HEAD_DIM:(hh - h0 + 1) * ATT_HEAD_DIM]
                t = t * lax.rsqrt(jnp.mean(t * t, axis=-1, keepdims=True) + EPS) * gain
                t = t * cos + _swap32(t) * sin
                t = t * scale if scale != 1.0 else t
                o_ref[:, hh * ATT_HEAD_DIM:(hh + 1) * ATT_HEAD_DIM] = t.astype(BF16)

    heads_to(q_ref, OFF_Q, N_Q_HEADS, qkg_ref[0:1, :], ATT_SCALE * LOG2E)
    heads_to(k_ref, OFF_K, N_KV_HEADS, qkg_ref[1:2, :], 1.0)

    proj_to(z_ref, OFF_Z, D_INNER)
    for g in range(SSD_GROUPS):
        xs_ref[g] = proj(OFF_XBC + g * GROUP_DIM, OFF_XBC + (g + 1) * GROUP_DIM).astype(BF16)
    bc = proj(OFF_XBC + D_INNER, OFF_Q).astype(BF16)
    for g in range(SSD_GROUPS):
        bm_ref[g] = bc[:, g * D_STATE:(g + 1) * D_STATE]
        cm_ref[g] = bc[:, BC_DIM + g * D_STATE:BC_DIM + (g + 1) * D_STATE]
    vv = proj(OFF_V, OFF_GS).astype(BF16)
    for hh in range(N_KV_HEADS):
        lo = 2 * hh * ATT_HEAD_DIM
        v_ref[:, lo:lo + ATT_HEAD_DIM] = vv[:, hh * ATT_HEAD_DIM:(hh + 1) * ATT_HEAD_DIM]
        v_ref[:, lo + ATT_HEAD_DIM:lo + 2 * ATT_HEAD_DIM] = jnp.ones((x_ref.shape[0], ATT_HEAD_DIM), BF16)
    proj_to(gs_ref, OFF_GS, D_MODEL)
    proj_to(ga_ref, OFF_GA, D_MODEL)
    dt_ref[...] = proj(OFF_DT, IN_COLS)


def inproj(x, layer, mods, row_of_b, norm_g, w_in, qk_g, cos_t, sin_t, tm):
    b, l, d = x.shape
    tok = lambda b, i: (b, i, 0)
    widths = (D_INNER, ATT_DIM, KV_DIM, 2 * KV_DIM, D_MODEL, D_MODEL, LANES)
    dtypes = (BF16, BF16, BF16, BF16, BF16, BF16, F32)
    g_widths = (GROUP_DIM, D_STATE, D_STATE)
    return pl.pallas_call(
        _inproj_kernel,
        grid=(b, l // tm),
        in_specs=[pl.BlockSpec((None, tm, d), tok),
                  _mod_spec(layer, row_of_b, 3), _mod_spec(layer, row_of_b, 4),
                  pl.BlockSpec((None, None, 1, d), lambda b, i: (layer, 1, 0, 0)),
                  pl.BlockSpec((None, d, IN_COLS), lambda b, i: (layer, 0, 0), pipeline_mode=pl.Buffered(1)),
                  _layer_spec(layer, 2, ATT_HEAD_DIM),
                  pl.BlockSpec((tm, ATT_HEAD_DIM), lambda b, i: (i, 0)),
                  pl.BlockSpec((tm, ATT_HEAD_DIM), lambda b, i: (i, 0))],
        out_specs=[pl.BlockSpec((None, tm, w), tok) for w in widths]
        + [pl.BlockSpec((None, SSD_GROUPS, tm, w), lambda b, i: (b, 0, i, 0)) for w in g_widths],
        out_shape=[jax.ShapeDtypeStruct((b, l, w), dt) for w, dt in zip(widths, dtypes)]
        + [jax.ShapeDtypeStruct((b, SSD_GROUPS, l, w), BF16) for w in g_widths],
        compiler_params=_cparams(("arbitrary", "arbitrary")),
        name="inproj",
    )(x, mods, mods, norm_g, w_in, qk_g, cos_t, sin_t)


def _scan_rows(v, suffix):
    n = v.shape[0]
    row = lax.broadcasted_iota(jnp.int32, v.shape, 0)
    k = 1
    while k < n:
        if suffix:
            v = v + jnp.where(row < n - k, pltpu.roll(v, n - k, axis=0), 0.0)
        else:
            v = v + jnp.where(row >= k, pltpu.roll(v, k, axis=0), 0.0)
        k *= 2
    return v


def _split_bf16(v, low, pieces):
    hi = v.astype(BF16).astype(F32)
    r1 = v - hi
    mid = r1.astype(BF16).astype(F32)
    first = jnp.where(low, hi, mid).astype(BF16)
    if pieces == 2:
        return first
    return jnp.concatenate([first, jnp.where(low, r1 - mid, 0.0).astype(BF16)], axis=1)


def _dtprep_kernel(raw_ref, bias_ref, nega_ref, cs_ref, e_ref, w_ref, rt_ref):
    n_chunks = raw_ref.shape[0] // CHUNK
    lane = lax.broadcasted_iota(jnp.int32, (CHUNK, LANES), 1)
    is_bwd = (lane & HEADS_PER_GROUP) != 0
    low = lane < PIECE
    for c in range(n_chunks):
        rows = slice(c * CHUNK, (c + 1) * CHUNK)
        v = raw_ref[rows, :] + bias_ref[...]
        dt = jnp.maximum(v, 0.0) + jnp.log1p(jnp.exp(-jnp.abs(v)))
        da = dt * nega_ref[...]
        pre = _scan_rows(da, suffix=False)
        cs = jnp.where(is_bwd, _scan_rows(da, suffix=True), pre)
        tot = pre[CHUNK - 1:CHUNK, :]
        cs_ref[rows, :] = _split_bf16(cs * LOG2E, low, 3)
        e_ref[rows, :] = _split_bf16(jnp.exp(cs), low, 2)
        w_ref[rows, :] = _split_bf16(dt * jnp.exp(tot - cs), low, 2)
        rt_ref[c] = ((cs - jnp.log(dt)) * LOG2E).T[0:PIECE, :]


def dtprep(raw, layer, bias_rows, nega_rows):
    b, l, _ = raw.shape
    r = min(l, 1024)
    tok = lambda b, i: (b, i, 0)
    return pl.pallas_call(
        _dtprep_kernel,
        grid=(b, l // r),
        in_specs=[pl.BlockSpec((None, r, LANES), tok), _layer_spec(layer, 1, LANES), _layer_spec(layer, 1, LANES)],
        out_specs=[pl.BlockSpec((None, r, 2 * LANES), tok), pl.BlockSpec((None, r, LANES), tok),
                   pl.BlockSpec((None, r, LANES), tok),
                   pl.BlockSpec((None, r // CHUNK, PIECE, CHUNK), lambda b, i: (b, i, 0, 0))],
        out_shape=[jax.ShapeDtypeStruct((b, l, 2 * LANES), BF16), jax.ShapeDtypeStruct((b, l, LANES), BF16),
                   jax.ShapeDtypeStruct((b, l, LANES), BF16),
                   jax.ShapeDtypeStruct((b, l // CHUNK, PIECE, CHUNK), F32)],
        compiler_params=_cparams(("arbitrary", "arbitrary")),
        name="dtprep",
    )(raw, bias_rows, nega_rows)


def _ssd_kernel(with_ctx_out,
                xs_l, bm_l, cm_l, cs_l, e_l, w_l, rt_l, xs_c, bm_c, cm_c, cs_c, e_c, w_c, rt_c,
                cwx, cwb, cwc, cbx, cbb, cbc, dskip, selcs, expf, expb,
                y_l, y_c,
                xc_l, bt_l, cc_l, sb_l, xc_c, bt_c, cc_c, sb_c, st_f, st_b):
    lat = dict(xs=xs_l, bm=bm_l, cm=cm_l, cs=cs_l, e=e_l, w=w_l, rt=rt_l, y=y_l,
               xc=xc_l, bt=bt_l, cc=cc_l, sb=sb_l, out=True)
    ctx = dict(xs=xs_c, bm=bm_c, cm=cm_c, cs=cs_c, e=e_c, w=w_c, rt=rt_c, y=y_c,
               xc=xc_c, bt=bt_c, cc=cc_c, sb=sb_c, out=with_ctx_out)

    def rows(c):
        return pl.ds(pl.multiple_of(c * CHUNK, CHUNK), CHUNK)

    def bwd_pass(s):
        n_chunks = s["xs"].shape[0] // CHUNK
        row = lax.broadcasted_iota(jnp.int32, (CHUNK, 1), 0)

        def conv(src, w_ref, b_ref, c):
            cur = src[rows(c), :].astype(F32)
            p_at = pl.multiple_of(jnp.maximum(c * CHUNK - 16, 0), 16)
            n_at = pl.multiple_of(jnp.minimum((c + 1) * CHUNK, n_chunks * CHUNK - 16), 16)
            prev = jnp.where(c > 0, src[pl.ds(p_at, 16), :].astype(F32)[15:16, :], 0.0)
            nxt = jnp.where(c < n_chunks - 1, src[pl.ds(n_at, 16), :].astype(F32)[0:1, :], 0.0)
            up = jnp.where(row == 0, prev, pltpu.roll(cur, 1, axis=0))
            dn = jnp.where(row == CHUNK - 1, nxt, pltpu.roll(cur, CHUNK - 1, axis=0))
            return _silu(w_ref[0:1, :] * up + w_ref[1:2, :] * cur + w_ref[2:3, :] * dn + b_ref[...])

        def body(i, carry):
            c = n_chunks - 1 - i
            xcv = conv(s["xs"], cwx, cbx, c)
            bt = conv(s["bm"], cwb, cbb, c).T.astype(BF16)
            s["xc"][rows(c), :] = xcv.astype(BF16)
            s["cc"][rows(c), :] = conv(s["cm"], cwc, cbc, c).astype(BF16)
            s["bt"][c] = bt
            s["sb"][c] = st_b[...].astype(BF16)
            w_b = _mm(s["w"][rows(c), :], expb[...])
            dec = _mm(s["e"][pl.ds(pl.multiple_of(c * CHUNK, CHUNK), 16), :], expb[...])[0:1, :]
            st_b[...] = st_b[...] * dec + _mm(bt, (xcv * w_b).astype(BF16))
            return carry

        lax.fori_loop(0, n_chunks, body, 0, unroll=2)

    def fwd_pass(s):
        n_chunks = s["xs"].shape[0] // CHUNK
        ri = lax.broadcasted_iota(jnp.int32, (CHUNK, CHUNK), 0)
        ci = lax.broadcasted_iota(jnp.int32, (CHUNK, CHUNK), 1)
        lower = ri >= ci
        upper = ri <= ci
        first = lax.broadcasted_iota(jnp.int32, (CHUNK, LANES), 1) < SSD_HEAD_DIM

        def body(c, carry):
            xc = s["xc"][rows(c), :]
            bt = s["bt"][c]
            if s["out"]:
                cc = s["cc"][rows(c), :]
                colb = _mm(s["cs"][rows(c), :], selcs[...])
                rt = s["rt"][c]
                cb = _mm(cc, bt)
                y_f = _mm(cc, st_f[...].astype(BF16))
                y_b = _mm(cc, s["sb"][c])
                blocks, dec = [], []
                for j in range(HEADS_PER_GROUP // 2):
                    col = [colb[:, (4 * j + t) * CHUNK:(4 * j + t + 1) * CHUNK] for t in range(4)]
                    ms = []
                    for t, hh in enumerate((2 * j, 2 * j + 1)):
                        hb = HEADS_PER_GROUP + hh
                        d_f = jnp.exp2(jnp.where(lower, col[t] - rt[hh:hh + 1, :], -jnp.inf))
                        d_b = jnp.exp2(jnp.where(upper, col[2 + t] - rt[hb:hb + 1, :], -jnp.inf))
                        ms.append((cb * (d_f + d_b)).astype(BF16))
                    xblk = xc[:, j * LANES:(j + 1) * LANES]
                    zero = jnp.zeros_like(xblk)
                    x_bd = jnp.concatenate([jnp.where(first, xblk, zero), jnp.where(first, zero, xblk)], axis=0)
                    yd = _mm(jnp.concatenate(ms, axis=1), x_bd)
                    e_f = jnp.exp2(jnp.where(first, col[0], col[1]))
                    e_b = jnp.exp2(jnp.where(first, col[2], col[3]))
                    sl = slice(j * LANES, (j + 1) * LANES)
                    blocks.append(yd + y_f[:, sl] * e_f + y_b[:, sl] * e_b + xblk.astype(F32) * dskip[:, sl])
                    dec.append(e_f[CHUNK - 1:CHUNK, :])
                s["y"][rows(c), :] = jnp.concatenate(blocks, axis=1).astype(BF16)
                dec = jnp.concatenate(dec, axis=1)
            else:
                last16 = pl.ds(pl.multiple_of(c * CHUNK + CHUNK - 16, 16), 16)
                dec = _mm(s["e"][last16, :], expf[...])[15:16, :]
            xw = (xc.astype(F32) * _mm(s["w"][rows(c), :], expf[...])).astype(BF16)
            st_f[...] = st_f[...] * dec + _mm(bt, xw)
            return carry

        lax.fori_loop(0, n_chunks, body, 0, unroll=2)

    st_b[...] = jnp.zeros_like(st_b)
    bwd_pass(ctx)
    bwd_pass(lat)
    st_f[...] = jnp.zeros_like(st_f)
    fwd_pass(ctx)
    fwd_pass(lat)
    if not with_ctx_out:
        y_c[...] = jnp.zeros_like(y_c)


def _head_select_tables():
    g = jnp.arange(SSD_GROUPS)[:, None, None]
    k = jnp.arange(2 * LANES)[None, :, None]
    n = jnp.arange(2 * HEADS_PER_GROUP * CHUNK)[None, None, :]
    q = n // CHUNK
    head_dir = (q % 4 // 2) * HEADS_PER_GROUP + 2 * (q // 4) + q % 2
    selcs = ((k % PIECE) == 2 * HEADS_PER_GROUP * g + head_dir) & (k < 3 * PIECE)
    k2 = jnp.arange(LANES)[None, :, None]
    n2 = jnp.arange(GROUP_DIM)[None, None, :]
    expf = (k2 % PIECE) == 2 * HEADS_PER_GROUP * g + n2 // SSD_HEAD_DIM
    expb = (k2 % PIECE) == 2 * HEADS_PER_GROUP * g + HEADS_PER_GROUP + n2 // SSD_HEAD_DIM
    return selcs.astype(BF16), expf.astype(BF16), expb.astype(BF16)


def ssd(conv_in_l, tabs_l, conv_in_c, tabs_c, layer, conv_w, conv_b, dskip_rows, tables, with_ctx_out):
    b, _, l, _ = conv_in_l[0].shape
    lc = conv_in_c[0].shape[2] // b
    n_bg = D_INNER // D_STATE
    slab = lambda b, g: (b, g, 0, 0)
    slab_c = lambda b, g: (0, g, b, 0)

    def seq_specs(n, slab):
        return [pl.BlockSpec((None, None, n, GROUP_DIM), slab),
                pl.BlockSpec((None, None, n, D_STATE), slab),
                pl.BlockSpec((None, None, n, D_STATE), slab),
                pl.BlockSpec((None, n, 2 * LANES), lambda b, g: (b, 0, 0)),
                pl.BlockSpec((None, n, LANES), lambda b, g: (b, 0, 0)),
                pl.BlockSpec((None, n, LANES), lambda b, g: (b, 0, 0)),
                pl.BlockSpec((None, n // CHUNK, 2 * HEADS_PER_GROUP, CHUNK), lambda b, g: (b, 0, g, 0))]

    def seq_scratch(n):
        return [pltpu.VMEM((n, GROUP_DIM), BF16), pltpu.VMEM((n // CHUNK, D_STATE, CHUNK), BF16),
                pltpu.VMEM((n, D_STATE), BF16), pltpu.VMEM((n // CHUNK, D_STATE, GROUP_DIM), BF16)]

    par_specs = [pl.BlockSpec((None, 3, GROUP_DIM), lambda b, g: (layer, 0, g)),
                 pl.BlockSpec((None, 3, D_STATE), lambda b, g: (layer, 0, n_bg + g)),
                 pl.BlockSpec((None, 3, D_STATE), lambda b, g: (layer, 0, n_bg + SSD_GROUPS + g)),
                 pl.BlockSpec((None, 1, GROUP_DIM), lambda b, g: (layer, 0, g)),
                 pl.BlockSpec((None, 1, D_STATE), lambda b, g: (layer, 0, n_bg + g)),
                 pl.BlockSpec((None, 1, D_STATE), lambda b, g: (layer, 0, n_bg + SSD_GROUPS + g)),
                 pl.BlockSpec((None, 1, GROUP_DIM), lambda b, g: (layer, 0, g)),
                 pl.BlockSpec((None, 2 * LANES, 2 * HEADS_PER_GROUP * CHUNK), lambda b, g: (g, 0, 0)),
                 pl.BlockSpec((None, LANES, GROUP_DIM), lambda b, g: (g, 0, 0)),
                 pl.BlockSpec((None, LANES, GROUP_DIM), lambda b, g: (g, 0, 0))]
    return pl.pallas_call(
        functools.partial(_ssd_kernel, with_ctx_out),
        grid=(b, SSD_GROUPS),
        in_specs=seq_specs(l, slab) + seq_specs(lc, slab_c) + par_specs,
        out_specs=[pl.BlockSpec((None, None, l, GROUP_DIM), slab),
                   pl.BlockSpec((None, None, lc, GROUP_DIM), slab_c)],
        out_shape=[jax.ShapeDtypeStruct((b, SSD_GROUPS, l, GROUP_DIM), BF16),
                   jax.ShapeDtypeStruct((1, SSD_GROUPS, b * lc, GROUP_DIM), BF16)],
        scratch_shapes=seq_scratch(l) + seq_scratch(lc) + [pltpu.VMEM((D_STATE, GROUP_DIM), F32)] * 2,
        compiler_params=_cparams(("arbitrary", "arbitrary")),
        name="ssd",
    )(*conv_in_l, *tabs_l, *conv_in_c, *tabs_c,
      conv_w, conv_w, conv_w, conv_b, conv_b, conv_b, dskip_rows, *tables)


def _attn_kernel(n_sets, q_ref, *refs):
    o_ref = refs[2 * n_sets]
    if n_sets == 1:
        k_ref, v_ref = refs[0], refs[1]
    else:
        k_ref, v_ref = refs[2 * n_sets + 1], refs[2 * n_sets + 2]

        @pl.when(pl.program_id(1) == 0)
        def _():
            at = 0
            for t in range(n_sets):
                n = refs[2 * t].shape[0]
                k_ref[at:at + n, :] = refs[2 * t][...]
                v_ref[at:at + n, :] = refs[2 * t + 1][...]
                at += n

    def scores(h):
        kh = h // Q_PER_KV
        return lax.dot_general(q_ref[:, h * ATT_HEAD_DIM:(h + 1) * ATT_HEAD_DIM],
                               k_ref[:, kh * ATT_HEAD_DIM:(kh + 1) * ATT_HEAD_DIM],
                               (((1,), (1,)), ((), ())), preferred_element_type=F32)

    def finish(h, s):
        kh = h // Q_PER_KV
        p = jnp.exp2(s - jnp.max(s, axis=-1, keepdims=True)).astype(BF16)
        o = _mm(p, v_ref[:, 2 * kh * ATT_HEAD_DIM:2 * (kh + 1) * ATT_HEAD_DIM])
        o_ref[:, h * ATT_HEAD_DIM:(h + 1) * ATT_HEAD_DIM] = (
            o[:, :ATT_HEAD_DIM] / o[:, ATT_HEAD_DIM:ATT_HEAD_DIM + 1]).astype(BF16)

    group = 4
    for h0 in range(0, N_Q_HEADS, group):
        ss = [scores(h) for h in range(h0, h0 + group)]
        for h, s in zip(range(h0, h0 + group), ss):
            finish(h, s)


def attention(q, kv_sets, tq):
    b, l, _ = q.shape
    whole = lambda b, i: (b, 0, 0)
    kv_specs, kv_args = [], []
    for k, v in kv_sets:
        kv_specs += [pl.BlockSpec((None, k.shape[1], KV_DIM), whole), pl.BlockSpec((None, k.shape[1], 2 * KV_DIM), whole)]
        kv_args += [k, v]
    nk = sum(k.shape[1] for k, _ in kv_sets)
    scratch = [] if len(kv_sets) == 1 else [pltpu.VMEM((nk, KV_DIM), BF16), pltpu.VMEM((nk, 2 * KV_DIM), BF16)]
    return pl.pallas_call(
        functools.partial(_attn_kernel, len(kv_sets)),
        grid=(b, l // tq),
        in_specs=[pl.BlockSpec((None, tq, ATT_DIM), lambda b, i: (b, i, 0))] + kv_specs,
        out_specs=pl.BlockSpec((None, tq, ATT_DIM), lambda b, i: (b, i, 0)),
        out_shape=jax.ShapeDtypeStruct((b, l, ATT_DIM), BF16),
        scratch_shapes=scratch,
        compiler_params=_cparams(("arbitrary", "arbitrary")),
        name="attention",
    )(q, *kv_args)


def _merge_kernel(y_ref, z_ref, sng_ref, att_ref, gs_ref, ga_ref, x_ref, g2_ref, wso_ref, wao_ref, wo_ref, o_ref):
    half = x_ref.shape[0] // 2
    for r in (slice(0, half), slice(half, 2 * half)):
        m = _sigmoid(ga_ref[r, :].astype(F32)) * _mm(att_ref[r, :], wao_ref[...])
        y_ssd = None
        for g in range(SSD_GROUPS):
            sl = slice(g * GROUP_DIM, (g + 1) * GROUP_DIM)
            yg = y_ref[g, r, :].astype(F32) * _silu(z_ref[r, sl].astype(F32))
            yn = yg * lax.rsqrt(jnp.mean(yg * yg, axis=-1, keepdims=True) + EPS) * sng_ref[:, sl]
            part = _mm(yn.astype(BF16), wso_ref[sl, :])
            y_ssd = part if y_ssd is None else y_ssd + part
        m = m + _sigmoid(gs_ref[r, :].astype(F32)) * y_ssd
        u = _mm(m.astype(BF16), wo_ref[...])
        o_ref[r, :] = x_ref[r, :] + g2_ref[...] * u


def merge(y, z, att, gs, ga, x, layer, mods, row_of_b, sng, wso, wao, wo, tm):
    b, l, d = x.shape
    tok = lambda b, i: (b, i, 0)
    return pl.pallas_call(
        _merge_kernel,
        grid=(b, l // tm),
        in_specs=[pl.BlockSpec((None, SSD_GROUPS, tm, GROUP_DIM), lambda b, i: (b, 0, i, 0)),
                  pl.BlockSpec((None, tm, D_INNER), tok),
                  _layer_spec(layer, 1, D_INNER), pl.BlockSpec((None, tm, ATT_DIM), tok),
                  pl.BlockSpec((None, tm, d), tok), pl.BlockSpec((None, tm, d), tok),
                  pl.BlockSpec((None, tm, d), tok), _mod_spec(layer, row_of_b, 5),
                  _layer_spec(layer, D_INNER, d), _layer_spec(layer, ATT_DIM, d), _layer_spec(layer, d, d)],
        out_specs=pl.BlockSpec((None, tm, d), tok),
        out_shape=jax.ShapeDtypeStruct(x.shape, F32),
        compiler_params=_cparams(("arbitrary", "arbitrary")),
        name="merge",
    )(y, z, sng, att, gs, ga, x, mods, wso, wao, wo)


def _rope_tables(n_tok):
    rows = n_tok // GRID_W
    row = jnp.broadcast_to(jnp.arange(rows)[:, None], (rows, GRID_W)).reshape(-1).astype(F32)
    col = jnp.broadcast_to(jnp.arange(GRID_W)[None, :], (rows, GRID_W)).reshape(-1).astype(F32)
    inv_freq = ROPE_THETA ** (-(jnp.arange(0, ROPE_AXIS_DIM, 2, dtype=F32) / ROPE_AXIS_DIM))
    ar = row[:, None] * inv_freq
    ac = col[:, None] * inv_freq
    cos_t = jnp.concatenate([jnp.cos(ar), jnp.cos(ar), jnp.cos(ac), jnp.cos(ac)], axis=1)
    sin_t = jnp.concatenate([-jnp.sin(ar), jnp.sin(ar), -jnp.sin(ac), jnp.sin(ac)], axis=1)
    return cos_t, sin_t


def _dt_lanes(fwd, bwd):
    lead = fwd.shape[:-1]
    f = fwd.reshape(*lead, SSD_GROUPS, 1, HEADS_PER_GROUP)
    bk = bwd.reshape(*lead, SSD_GROUPS, 1, HEADS_PER_GROUP)
    half = jnp.concatenate([f, bk], axis=-2).reshape(*lead, PIECE)
    return jnp.concatenate([half, half], axis=-1)


def _permute_w_in(w_in):
    o = D_INNER + CONV_DIM
    w_in = w_in.astype(BF16)
    dt_cols = _dt_lanes(w_in[..., o:o + SSD_HEADS], w_in[..., o + SSD_HEADS:o + 2 * SSD_HEADS])
    return jnp.concatenate([w_in[..., :o], w_in[..., o + 2 * SSD_HEADS:], dt_cols], axis=-1)


def kernel(x, c, ctx, c_ctx, w_ada, b_ada, norm_g, ffn1_w13, ffn1_w2, w_in, conv_w, conv_b,
           a_log, dt_bias, d_skip, ssd_norm_g, w_ssd_out, qk_norm_g, w_attn_out, w_out,
           ffn2_w13, ffn2_w2):
    batch, seq, d = x.shape
    lc = ctx.shape[1]
    tm_l = tm_c = 512
    assert batch + 1 <= MOD_ROWS and seq % tm_l == 0 and lc % CHUNK == 0 and (batch * lc) % tm_c == 0

    cond = jnp.zeros((MOD_ROWS, d), F32).at[:batch].set(c).at[batch].set(c_ctx)
    mods = ada_all(cond, w_ada, b_ada).reshape(DEPTH, MOD_ROWS, 1, N_MOD * d)

    cos_l, sin_l = _rope_tables(seq)
    cos_c = jnp.ones((batch * lc, ATT_HEAD_DIM), F32)
    sin_c = jnp.zeros((batch * lc, ATT_HEAD_DIM), F32)

    w_in_p = _permute_w_in(w_in)
    f1_w13, f1_w2 = ffn1_w13.astype(BF16), ffn1_w2.astype(BF16)
    f2_w13, f2_w2 = ffn2_w13.astype(BF16), ffn2_w2.astype(BF16)
    wso, wao, wo = w_ssd_out.astype(BF16), w_attn_out.astype(BF16), w_out.astype(BF16)
    dtb_rows = _dt_lanes(dt_bias[:, 0], dt_bias[:, 1]).reshape(DEPTH, 1, LANES)
    nega = -jnp.exp(a_log.astype(F32))
    nega_rows = _dt_lanes(nega[:, 0], nega[:, 1]).reshape(DEPTH, 1, LANES)
    dskip_rows = jnp.repeat(d_skip, SSD_HEAD_DIM, axis=-1).reshape(DEPTH, 1, D_INNER)
    norm_g = norm_g.reshape(DEPTH, 3, 1, d)
    sng = ssd_norm_g.reshape(DEPTH, 1, D_INNER)
    conv_b3 = conv_b.reshape(DEPTH, 1, CONV_DIM)
    tables = _head_select_tables()

    lat_row = lambda b: b
    ctx_row = lambda b: batch
    xc = ctx.reshape(1, batch * lc, d)
    per_batch = lambda a: a.reshape(batch, lc, a.shape[-1])
    for i in range(DEPTH):
        last = i == DEPTH - 1
        x = ffn(x, i, mods, lat_row, 0, norm_g, f1_w13, f1_w2, tm_l)
        xc = ffn(xc, i, mods, ctx_row, 0, norm_g, f1_w13, f1_w2, tm_c)

        z, q, k, v, gs, ga, dt, *conv_in = inproj(x, i, mods, lat_row, norm_g, w_in_p, qk_norm_g, cos_l, sin_l, tm_l)
        zc, qc, kc, vc, gsc, gac, dtc, *conv_in_c = inproj(xc, i, mods, ctx_row, norm_g, w_in_p, qk_norm_g,
                                                           cos_c, sin_c, tm_c)
        tabs = dtprep(dt, i, dtb_rows, nega_rows)
        tabs_c = dtprep(per_batch(dtc), i, dtb_rows, nega_rows)
        y, yc = ssd(conv_in, tabs, conv_in_c, tabs_c, i, conv_w, conv_b3, dskip_rows, tables, not last)
        kc, vc = per_batch(kc), per_batch(vc)
        att = attention(q, [(kc, vc), (k, v)], 256)
        x = merge(y, z, att, gs, ga, x, i, mods, lat_row, sng, wso, wao, wo, tm_l)
        x = ffn(x, i, mods, lat_row, 6, norm_g, f2_w13, f2_w2, tm_l)
        if not last:
            attc = attention(per_batch(qc), [(kc, vc)], lc).reshape(1, batch * lc, ATT_DIM)
            xc = merge(yc, zc, attc, gsc, gac, xc, i, mods, ctx_row, sng, wso, wao, wo, tm_c)
            xc = ffn(xc, i, mods, ctx_row, 6, norm_g, f2_w13, f2_w2, tm_c)
    return x
```
